```python
import jax, jax.numpy as jnp
from jax import lax
import numpy as np

D_MODEL = 4096
BATCH = 4
SEQ = 2048
DEPTH = 2
DEC_BATCH = 8
DEC_SEQ = 2048
PAST_LEN = 128

GRID_W = 64
ROPE_THETA = 10000.0
Q_BLOCK = 128
EPS = 1e-6

MLA_HEADS = 16
MLA_NOPE = 128
MLA_ROPE = 64
MLA_V = 128
Q_LORA = 1024
KV_LORA = 512
MLA_WIDTH = MLA_HEADS * MLA_V

FNET_GROUPS = 4
FNET_GROUP_DIM = 512
FNET_WIDTH = FNET_GROUPS * FNET_GROUP_DIM

GQA_HEADS = 16
GQA_KV_HEADS = 4
GQA_HEAD_DIM = 128
GQA_WIDTH = GQA_HEADS * GQA_HEAD_DIM

N_BRANCH = 3
BRANCH_WIDTH = 2048
D_FF = 4 * D_MODEL

_SEG = [Q_LORA, KV_LORA, MLA_ROPE, FNET_WIDTH, GQA_WIDTH,
        GQA_KV_HEADS * GQA_HEAD_DIM, GQA_KV_HEADS * GQA_HEAD_DIM, N_BRANCH * D_MODEL]
SPLITS = [int(v) for v in np.cumsum(_SEG)[:-1]]
IN_WIDTH = int(sum(_SEG))

kernel_name = 'hybrid_mla_fnet_gqa_encoder'


def rms_norm(x, g):
    xf = x.astype(jnp.float32)
    y = xf * lax.rsqrt(jnp.mean(xf * xf, axis=-1, keepdims=True) + EPS)
    return y.astype(x.dtype) * g


def axial_rope(n_tok, rot_dim):
    rows = n_tok // GRID_W
    row_idx = jnp.broadcast_to(jnp.arange(rows)[:, None], (rows, GRID_W)).reshape(-1).astype(jnp.float32)
    col_idx = jnp.broadcast_to(jnp.arange(GRID_W)[None, :], (rows, GRID_W)).reshape(-1).astype(jnp.float32)
    nq = rot_dim // 4
    freqs = ROPE_THETA ** (-(2.0 * jnp.arange(nq, dtype=jnp.float32)) / (rot_dim // 2))
    ang = jnp.stack([row_idx[:, None] * freqs, col_idx[:, None] * freqs], axis=1)
    return jnp.cos(ang), jnp.sin(ang)


def apply_rope(x, cos, sin):
    r = x.shape[-1]
    xs = x.reshape(*x.shape[:-1], 2, 2, r // 4)
    x1, x2 = xs[..., 0, :], xs[..., 1, :]
    c = cos[None, :, None].astype(x.dtype)
    s = sin[None, :, None].astype(x.dtype)
    out = jnp.stack([x1 * c - x2 * s, x2 * c + x1 * s], axis=-2)
    return out.reshape(x.shape)


def blocked_attention(q, k, v, scale):
    b, n, h, d = q.shape
    hk = k.shape[2]
    g = h // hk
    dv = v.shape[-1]
    qb = q.reshape(b, n // Q_BLOCK, Q_BLOCK, hk, g, d).transpose(1, 0, 2, 3, 4, 5)

    def one_block(q_blk):
        s = jnp.einsum('bqhgd,bkhd->bhgqk', q_blk, k).astype(jnp.float32) * scale
        p = jax.nn.softmax(s, axis=-1).astype(v.dtype)
        return jnp.einsum('bhgqk,bkhd->bqhgd', p, v)

    o = lax.map(one_block, qb)
    return o.transpose(1, 0, 2, 3, 4, 5).reshape(b, n, h * dv)


def encoder_layer(x, rope_a, rope_c, w_in, g_attn, g_qa, w_uq, g_kva, w_ukv,
                  g_qk_q, g_qk_k, w_branch, w_o, g_mlp, w_up, w_down):
    b, n, _ = x.shape
    cos_a, sin_a = rope_a
    cos_c, sin_c = rope_c
    h = rms_norm(x, g_attn)
    z = h @ w_in
    c_q, c_kv, k_pe, u_f, q_c, k_c, v_c, gate_logits = jnp.split(z, SPLITS, axis=-1)

    q_a = (rms_norm(c_q, g_qa) @ w_uq).reshape(b, n, MLA_HEADS, MLA_NOPE + MLA_ROPE)
    q_nope, q_pe = q_a[..., :MLA_NOPE], q_a[..., MLA_NOPE:]
    kv = (rms_norm(c_kv, g_kva) @ w_ukv).reshape(b, n, MLA_HEADS, MLA_NOPE + MLA_V)
    k_nope, v_a = kv[..., :MLA_NOPE], kv[..., MLA_NOPE:]
    q_pe = apply_rope(q_pe, cos_a, sin_a)
    k_pe = apply_rope(k_pe.reshape(b, n, 1, MLA_ROPE), cos_a, sin_a)
    q_full = jnp.concatenate([q_nope, q_pe], axis=-1)
    k_full = jnp.concatenate([k_nope, jnp.broadcast_to(k_pe, (b, n, MLA_HEADS, MLA_ROPE))], axis=-1)
    o_a = blocked_attention(q_full, k_full, v_a, (MLA_NOPE + MLA_ROPE) ** -0.5)

    u = u_f.reshape(b, n, FNET_GROUPS, FNET_GROUP_DIM).astype(jnp.float32)
    o_b = jnp.fft.fft2(u, axes=(1, 3), norm='ortho').real.astype(x.dtype).reshape(b, n, FNET_WIDTH)

    q = apply_rope(rms_norm(q_c.reshape(b, n, GQA_HEADS, GQA_HEAD_DIM), g_qk_q), cos_c, sin_c)
    k = apply_rope(rms_norm(k_c.reshape(b, n, GQA_KV_HEADS, GQA_HEAD_DIM), g_qk_k), cos_c, sin_c)
    v = v_c.reshape(b, n, GQA_KV_HEADS, GQA_HEAD_DIM)
    o_c = blocked_attention(q, k, v, GQA_HEAD_DIM ** -0.5)

    gates = jax.nn.sigmoid(gate_logits.astype(jnp.float32)).astype(x.dtype).reshape(b, n, N_BRANCH, D_MODEL)
    branches = (o_a, o_b, o_c)
    merged = gates[:, :, 0] * (branches[0] @ w_branch[0])
    for i in range(1, N_BRANCH):
        merged = merged + gates[:, :, i] * (branches[i] @ w_branch[i])
    x = x + merged @ w_o

    h2 = rms_norm(x, g_mlp)
    x = x + jnp.square(jax.nn.relu(h2 @ w_up)) @ w_down
    return x


def trunk(x, w_in, g_attn, g_qa, w_uq, g_kva, w_ukv, g_qk_q, g_qk_k,
          w_branch, w_o, g_mlp, w_up, w_down, g_final):
    n = x.shape[1]
    rope_a = axial_rope(n, MLA_ROPE)
    rope_c = axial_rope(n, GQA_HEAD_DIM)
    for l in range(DEPTH):
        x = encoder_layer(x, rope_a, rope_c, w_in[l], g_attn[l], g_qa[l], w_uq[l], g_kva[l],
                          w_ukv[l], g_qk_q[l], g_qk_k[l], w_branch[l], w_o[l], g_mlp[l],
                          w_up[l], w_down[l])
    return rms_norm(x, g_final)


def setup_inputs(seed: int = 0) -> dict:
    key = jax.random.key(seed)
    ks = jax.random.split(key, 20)
    f32 = jnp.float32

    def w(k, shape, fan_in):
        return jax.random.normal(k, shape, f32) * (fan_in ** -0.5)

    def gain(k, shape):
        return 1.0 + 0.02 * jax.random.normal(k, shape, f32)

    return {
        'x_prompt': jax.random.normal(ks[0], (BATCH, SEQ, D_MODEL), f32),
        'x_sample': jax.random.normal(ks[1], (DEC_BATCH, DEC_SEQ, D_MODEL), f32),
        'w_in': w(ks[2], (DEPTH, D_MODEL, IN_WIDTH), D_MODEL),
        'g_attn': gain(ks[3], (DEPTH, D_MODEL)),
        'g_qa': gain(ks[4], (DEPTH, Q_LORA)),
        'w_uq': w(ks[5], (DEPTH, Q_LORA, MLA_HEADS * (MLA_NOPE + MLA_ROPE)), Q_LORA),
        'g_kva': gain(ks[6], (DEPTH, KV_LORA)),
        'w_ukv': w(ks[7], (DEPTH, KV_LORA, MLA_HEADS * (MLA_NOPE + MLA_V)), KV_LORA),
        'g_qk_q': gain(ks[8], (DEPTH, GQA_HEAD_DIM)),
        'g_qk_k': gain(ks[9], (DEPTH, GQA_HEAD_DIM)),
        'w_branch': w(ks[10], (DEPTH, N_BRANCH, BRANCH_WIDTH, D_MODEL), BRANCH_WIDTH),
        'w_o': w(ks[11], (DEPTH, D_MODEL, D_MODEL), D_MODEL),
        'g_mlp': gain(ks[12], (DEPTH, D_MODEL)),
        'w_up': w(ks[13], (DEPTH, D_MODEL, D_FF), D_MODEL),
        'w_down': w(ks[14], (DEPTH, D_FF, D_MODEL), D_FF),
        'g_final': gain(ks[15], (D_MODEL,)),
    }


def reference(x_prompt, x_sample, w_in, g_attn, g_qa, w_uq, g_kva, w_ukv, g_qk_q, g_qk_k,
              w_branch, w_o, g_mlp, w_up, w_down, g_final):
    y_prompt = trunk(x_prompt, w_in, g_attn, g_qa, w_uq, g_kva, w_ukv, g_qk_q, g_qk_k,
                     w_branch, w_o, g_mlp, w_up, w_down, g_final)
    y_sample = trunk(x_sample, w_in, g_attn, g_qa, w_uq, g_kva, w_ukv, g_qk_q, g_qk_k,
                     w_branch, w_o, g_mlp, w_up, w_down, g_final)
    return (y_prompt, y_sample)
```

```python
import functools
import math

import jax
import jax.numpy as jnp
from jax import lax
from jax.experimental import pallas as pl
from jax.experimental.pallas import tpu as pltpu

F32 = jnp.float32
BF16 = jnp.bfloat16

GRID_W = 64
ROPE_THETA = 10000.0
EPS = 1e-6
MLA_HEADS = 16
MLA_NOPE = 128
MLA_ROPE = 64
MLA_V = 128
FNET_GROUPS = 4
GQA_HEADS = 16
GQA_KV_HEADS = 4
GQA_HEAD_DIM = 128
N_BRANCH = 3

LANES = 128
MLA_QK_PAD = 2 * LANES
VMEM_LIMIT_BYTES = 56 * 2 ** 20


def _tile(n, pref):
    if n <= pref:
        return n
    t = (pref // LANES) * LANES
    while t > LANES and n % t:
        t -= LANES
    assert n % t == 0, (n, pref)
    return t


def _params(*sem):
    return pltpu.CompilerParams(dimension_semantics=sem, vmem_limit_bytes=VMEM_LIMIT_BYTES)


def _rmsnorm_kernel(x_ref, g_ref, o_ref):
    x = x_ref[...]
    ms = jnp.mean(x * x, axis=-1, keepdims=True)
    o_ref[...] = ((x * lax.rsqrt(ms + EPS)) * g_ref[...]).astype(o_ref.dtype)


def _rmsnorm(x, g, out_dtype, name):
    t, d = x.shape
    tm = _tile(t, 256)
    return pl.pallas_call(
        _rmsnorm_kernel,
        grid=(t // tm,),
        in_specs=[pl.BlockSpec((tm, d), lambda i: (i, 0)),
                  pl.BlockSpec((1, d), lambda i: (0, 0))],
        out_specs=pl.BlockSpec((tm, d), lambda i: (i, 0)),
        out_shape=jax.ShapeDtypeStruct((t, d), out_dtype),
        compiler_params=_params("parallel"),
        name=name,
    )(x, g.reshape(1, d))


def _mm_kernel(a_ref, w_ref, *refs, epi, n_extra, n_out, nk):
    extras = refs[:n_extra]
    outs = refs[n_extra:n_extra + n_out]
    if nk == 1:
        acc = jnp.dot(a_ref[...], w_ref[...], preferred_element_type=F32)
        epi(acc, extras, outs)
        return
    acc_ref = refs[n_extra + n_out]
    k = pl.program_id(2)

    @pl.when(k == 0)
    def _():
        acc_ref[...] = jnp.zeros_like(acc_ref)

    acc_ref[...] += jnp.dot(a_ref[...], w_ref[...], preferred_element_type=F32)

    @pl.when(k == nk - 1)
    def _():
        epi(acc_ref[...], extras, outs)


def _mm(a, w, epi, outs, extras=(), *, tm, tn, tk=None, name):
    m, kdim = a.shape
    n = w.shape[1]
    tm = _tile(m, tm)
    tn = _tile(n, tn)
    tk = kdim if tk is None else _tile(kdim, tk)
    nk = kdim // tk
    grid = (m // tm, n // tn, nk)
    in_specs = [pl.BlockSpec((tm, tk), lambda i, j, k: (i, k)),
                pl.BlockSpec((tk, tn), lambda i, j, k: (k, j))]
    for _, bshape, imap in extras:
        in_specs.append(pl.BlockSpec(bshape, lambda i, j, k, imap=imap: imap(i, j)))
    out_specs = [pl.BlockSpec((tm, bc), lambda i, j, k: (i, j)) for _, bc, _ in outs]
    out_shape = [jax.ShapeDtypeStruct((m, tc), dt) for tc, _, dt in outs]
    scratch = [pltpu.VMEM((tm, tn), F32)] if nk > 1 else []
    res = pl.pallas_call(
        functools.partial(_mm_kernel, epi=epi, n_extra=len(extras), n_out=len(outs), nk=nk),
        grid=grid,
        in_specs=in_specs,
        out_specs=out_specs,
        out_shape=out_shape,
        scratch_shapes=scratch,
        compiler_params=_params("parallel", "parallel", "arbitrary"),
        name=name,
    )(a, w, *[e[0] for e in extras])
    return res


def _epi_cast(acc, extras, outs):
    outs[0][...] = acc.astype(outs[0].dtype)


def _epi_relu2(acc, extras, outs):
    r = jnp.maximum(acc, 0.0)
    outs[0][...] = (r * r).astype(outs[0].dtype)


def _epi_residual(acc, extras, outs):
    outs[0][...] = extras[0][...] + acc


def _rms(x, g):
    return (x * lax.rsqrt(jnp.mean(x * x, axis=-1, keepdims=True) + EPS)) * g


def _rope(x, c, sa, sb, half):
    return x * c + pltpu.roll(x, LANES - half, 1) * sa + pltpu.roll(x, half, 1) * sb


def _epi_latent(acc, extras, outs, *, q_lora, kv_lora):
    gq, gkv, c, sa, sb = extras
    outs[0][...] = _rms(acc[:, :q_lora], gq[...]).astype(BF16)
    outs[1][...] = _rms(acc[:, q_lora:q_lora + kv_lora], gkv[...]).astype(BF16)
    pe = acc[:, q_lora + kv_lora:q_lora + kv_lora + LANES]
    outs[2][...] = _rope(pe, c[...], sa[...], sb[...], MLA_ROPE // 4).astype(BF16)


def _epi_gqa_heads(acc, extras, outs):
    g, c, sa, sb = extras
    cv, sav, sbv = c[...], sa[...], sb[...]
    for h in range(acc.shape[1] // GQA_HEAD_DIM):
        sl = slice(h * GQA_HEAD_DIM, (h + 1) * GQA_HEAD_DIM)
        y = _rms(acc[:, sl], g[:, sl])
        outs[0][:, sl] = _rope(y, cv, sav, sbv, GQA_HEAD_DIM // 4).astype(BF16)


def _epi_mla_q(acc, extras, outs, *, scale):
    c, sa, sb = extras
    cv, sav, sbv = c[...], sa[...], sb[...]
    for h in range(acc.shape[1] // MLA_QK_PAD):
        lo = slice(h * MLA_QK_PAD, h * MLA_QK_PAD + LANES)
        hi = slice(h * MLA_QK_PAD + LANES, (h + 1) * MLA_QK_PAD)
        outs[0][:, lo] = (acc[:, lo] * scale).astype(BF16)
        outs[0][:, hi] = _rope(acc[:, hi], cv, sav, sbv, MLA_ROPE // 4).astype(BF16)


def _attn_kernel(q_ref, *refs, has_kpe):
    if has_kpe:
        k1_ref, k2_ref, v_ref, o_ref = refs
        k = jnp.concatenate([k1_ref[...], k2_ref[...]], axis=-1)
    else:
        k_ref, v_ref, o_ref = refs
        k = k_ref[...]
    s = lax.dot_general(q_ref[...], k, (((1,), (1,)), ((), ())), preferred_element_type=F32)
    m = jnp.max(s, axis=-1, keepdims=True)
    p = jnp.exp(s - m)
    l = jnp.sum(p, axis=-1, keepdims=True)
    o = jnp.dot(p.astype(BF16), v_ref[...], preferred_element_type=F32)
    o_ref[...] = (o / l).astype(o_ref.dtype)


def _attention(q, q_spec, ks, v, v_spec, *, n_batch, n_tok, n_heads, dv, name):
    tq = _tile(n_tok, 512)
    nq = n_tok // tq
    q_cols, q_col0 = q_spec
    in_specs = [pl.BlockSpec((tq, q_cols), lambda b, h, i: (b * nq + i, q_col0 + h))]
    args = [q]
    for arr, cols, colfn in ks:
        in_specs.append(pl.BlockSpec((n_tok, cols), lambda b, h, i, colfn=colfn: (b, colfn(h))))
        args.append(arr)
    in_specs.append(pl.BlockSpec((n_tok, dv), lambda b, h, i: (b, v_spec(h))))
    args.append(v)
    return pl.pallas_call(
        functools.partial(_attn_kernel, has_kpe=len(ks) == 2),
        grid=(n_batch, n_heads, nq),
        in_specs=in_specs,
        out_specs=pl.BlockSpec((tq, dv), lambda b, h, i: (b * nq + i, h)),
        out_shape=jax.ShapeDtypeStruct((n_batch * n_tok, n_heads * dv), BF16),
        compiler_params=_params("parallel", "parallel", "parallel"),
        name=name,
    )(*args)


def _fnet_chan_kernel(u_ref, w_ref, yc_ref, ys_ref):
    gd = yc_ref.shape[1]
    y = jnp.dot(u_ref[...], w_ref[...], preferred_element_type=F32)
    yc_ref[...] = y[:, :gd].astype(BF16)
    ys_ref[...] = y[:, gd:].astype(BF16)


def _fnet_pos_kernel(cn_ref, sn_ref, yc_ref, ys_ref, o_ref):
    o = jnp.dot(cn_ref[...], yc_ref[...], preferred_element_type=F32)
    o = o + jnp.dot(sn_ref[...], ys_ref[...], preferred_element_type=F32)
    o_ref[...] = o.astype(o_ref.dtype)


def _dft_tables(n):
    idx = jnp.arange(n, dtype=jnp.int32)
    jk = (idx[:, None] * idx[None, :]) % n
    ang = jk.astype(F32) * (2.0 * math.pi / n)
    return jnp.cos(ang), jnp.sin(ang)


def _fnet(z, u_col0, *, n_batch, n_tok, gd):
    t = z.shape[0]
    width = FNET_GROUPS * gd
    norm = 1.0 / math.sqrt(n_tok * gd)
    s_chan = 2.0 ** round(math.log2(norm) / 2)
    s_pos = norm / s_chan
    cc, sc = _dft_tables(gd)
    w_chan = (jnp.concatenate([cc, sc], axis=1) * s_chan).astype(BF16)
    cn, sn = _dft_tables(n_tok)
    cn = (cn * s_pos).astype(BF16)
    msn = (sn * (-s_pos)).astype(BF16)

    tm = _tile(t, 1024)
    ublk = u_col0 // gd
    assert u_col0 % gd == 0
    yc, ys = pl.pallas_call(
        _fnet_chan_kernel,
        grid=(t // tm, FNET_GROUPS),
        in_specs=[pl.BlockSpec((tm, gd), lambda i, g: (i, ublk + g)),
                  pl.BlockSpec((gd, 2 * gd), lambda i, g: (0, 0))],
        out_specs=[pl.BlockSpec((tm, gd), lambda i, g: (i, g))] * 2,
        out_shape=[jax.ShapeDtypeStruct((t, width), BF16)] * 2,
        compiler_params=_params("parallel", "parallel"),
        name="fnet_chan",
    )(z, w_chan)

    tp = _tile(n_tok, 1024)
    tn = _tile(width, 1024)
    npb = n_tok // tp
    return pl.pallas_call(
        _fnet_pos_kernel,
        grid=(n_batch, width // tn, npb),
        in_specs=[pl.BlockSpec((tp, n_tok), lambda b, j, i: (i, 0)),
                  pl.BlockSpec((tp, n_tok), lambda b, j, i: (i, 0)),
                  pl.BlockSpec((n_tok, tn), lambda b, j, i: (b, j)),
                  pl.BlockSpec((n_tok, tn), lambda b, j, i: (b, j))],
        out_specs=pl.BlockSpec((tp, tn), lambda b, j, i: (b * npb + i, j)),
        out_shape=jax.ShapeDtypeStruct((t, width), BF16),
        compiler_params=_params("parallel", "parallel", "parallel"),
        name="fnet_pos",
    )(cn, msn, yc, ys)


def _merge_kernel(oa_ref, ob_ref, oc_ref, w_ref, g_ref, out_ref, acc_ref):
    br = pl.program_id(2)
    gate = 1.0 / (1.0 + jnp.exp(-g_ref[...].astype(F32)))

    def contrib(o_ref):
        return gate * jnp.dot(o_ref[...], w_ref[0], preferred_element_type=F32)

    @pl.when(br == 0)
    def _():
        acc_ref[...] = contrib(oa_ref)

    @pl.when(br == 1)
    def _():
        acc_ref[...] += contrib(ob_ref)

    @pl.when(br == 2)
    def _():
        out_ref[...] = (acc_ref[...] + contrib(oc_ref)).astype(out_ref.dtype)


def _merge(oa, ob, oc, w_branch, z, gate_col0, d_model):
    t, bw = oa.shape
    tm = _tile(t, 1024)
    tn = _tile(d_model, 512)
    nj = d_model // tn
    assert gate_col0 % tn == 0
    g0 = gate_col0 // tn
    o_spec = pl.BlockSpec((tm, bw), lambda i, j, br: (i, 0))
    return pl.pallas_call(
        _merge_kernel,
        grid=(t // tm, nj, N_BRANCH),
        in_specs=[o_spec, o_spec, o_spec,
                  pl.BlockSpec((1, bw, tn), lambda i, j, br: (br, 0, j)),
                  pl.BlockSpec((tm, tn), lambda i, j, br: (i, g0 + br * nj + j))],
        out_specs=pl.BlockSpec((tm, tn), lambda i, j, br: (i, j)),
        out_shape=jax.ShapeDtypeStruct((t, d_model), BF16),
        scratch_shapes=[pltpu.VMEM((tm, tn), F32)],
        compiler_params=_params("parallel", "parallel", "arbitrary"),
        name="gated_merge",
    )(oa, ob, oc, w_branch, z)


def _rope_tables(n_tok, rot_dim, scale):
    rows = n_tok // GRID_W
    row_idx = jnp.broadcast_to(jnp.arange(rows)[:, None], (rows, GRID_W)).reshape(-1).astype(F32)
    col_idx = jnp.broadcast_to(jnp.arange(GRID_W)[None, :], (rows, GRID_W)).reshape(-1).astype(F32)
    nq = rot_dim // 4
    freqs = ROPE_THETA ** (-(2.0 * jnp.arange(nq, dtype=F32)) / (rot_dim // 2))
    ang = jnp.stack([row_idx[:, None] * freqs, col_idx[:, None] * freqs], axis=1)
    cos, sin = jnp.cos(ang), jnp.sin(ang)
    zero = jnp.zeros_like(cos[:, 0])
    pad = jnp.zeros((n_tok, LANES - rot_dim), F32)
    c = jnp.concatenate([cos[:, 0], cos[:, 0], cos[:, 1], cos[:, 1], pad], axis=1)
    sa = jnp.concatenate([-sin[:, 0], zero, -sin[:, 1], zero, pad], axis=1)
    sb = jnp.concatenate([zero, sin[:, 0], zero, sin[:, 1], pad], axis=1)
    return c * scale, sa * scale, sb * scale


def _layer_weights(w_in, w_uq, w_ukv, g_qk_q, g_qk_k, q_lora, kv_lora, d_model, fw, gqa_scale):
    seg = [q_lora, kv_lora, MLA_ROPE, fw, GQA_HEADS * GQA_HEAD_DIM,
           GQA_KV_HEADS * GQA_HEAD_DIM, GQA_KV_HEADS * GQA_HEAD_DIM, N_BRANCH * d_model]
    offs = [0]
    for s in seg:
        offs.append(offs[-1] + s)
    cq, ckv, kpe, uf, qc, kc, vc, gl = [w_in[:, offs[i]:offs[i + 1]] for i in range(8)]
    w_plain = jnp.concatenate([gl, uf, vc], axis=1).astype(BF16)
    w_qk = jnp.concatenate([qc, kc], axis=1).astype(BF16)
    kpad = jnp.zeros((w_in.shape[0], LANES - MLA_ROPE), w_in.dtype)
    w_lat = jnp.concatenate([cq, ckv, kpe, kpad], axis=1).astype(BF16)
    g_qk = jnp.concatenate([jnp.tile(g_qk_q * gqa_scale, GQA_HEADS),
                            jnp.tile(g_qk_k, GQA_KV_HEADS)]).reshape(1, -1)
    uq = w_uq.reshape(q_lora, MLA_HEADS, MLA_NOPE + MLA_ROPE)
    uq = jnp.pad(uq, ((0, 0), (0, 0), (0, MLA_QK_PAD - MLA_NOPE - MLA_ROPE)))
    w_uq_p = uq.reshape(q_lora, MLA_HEADS * MLA_QK_PAD).astype(BF16)
    ukv = w_ukv.reshape(kv_lora, MLA_HEADS, MLA_NOPE + MLA_V)
    w_ukv_p = jnp.concatenate([ukv[:, :, :MLA_NOPE].reshape(kv_lora, -1),
                               ukv[:, :, MLA_NOPE:].reshape(kv_lora, -1)], axis=1).astype(BF16)
    return w_plain, w_qk, w_lat, g_qk, w_uq_p, w_ukv_p


def _layer(x, n_batch, n_tok, tabs, w_in, g_attn, g_qa, w_uq, g_kva, w_ukv, g_qk_q, g_qk_k,
           w_branch, w_o, g_mlp, w_up, w_down):
    t, d_model = x.shape
    q_lora, kv_lora = g_qa.shape[0], g_kva.shape[0]
    fw = w_branch.shape[1]
    gd = fw // FNET_GROUPS
    mla_scale = (MLA_NOPE + MLA_ROPE) ** -0.5
    gqa_scale = GQA_HEAD_DIM ** -0.5
    (ca, saa, sba), (ca_s, saa_s, sba_s), (cc, sac, sbc) = tabs
    w_plain, w_qk, w_lat, g_qk, w_uq_p, w_ukv_p = _layer_weights(
        w_in, w_uq, w_ukv, g_qk_q, g_qk_k, q_lora, kv_lora, d_model, fw, gqa_scale)

    tm = _tile(n_tok, 1024)
    nrb = n_tok // tm
    tab_map = lambda i, j: (i % nrb, 0)

    h = _rmsnorm(x, g_attn, BF16, "rmsnorm_attn")

    gate_col0, u_col0, v_col0 = 0, N_BRANCH * d_model, N_BRANCH * d_model + fw
    (z_plain,) = _mm(h, w_plain, _epi_cast, [(w_plain.shape[1], _tile(w_plain.shape[1], 1024), BF16)],
                     tm=tm, tn=1024, name="in_proj_plain")
    qk_tn = _tile(w_qk.shape[1], 512)
    (qk,) = _mm(h, w_qk, _epi_gqa_heads, [(w_qk.shape[1], qk_tn, BF16)],
                extras=[(g_qk, (1, qk_tn), lambda i, j: (0, j)),
                        (cc, (tm, LANES), tab_map), (sac, (tm, LANES), tab_map), (sbc, (tm, LANES), tab_map)],
                tm=tm, tn=512, name="in_proj_gqa_qk")
    tml = _tile(n_tok, 512)
    nrl = n_tok // tml
    lat_map = lambda i, j: (i % nrl, 0)
    cqn, ckvn, kpe = _mm(
        h, w_lat, functools.partial(_epi_latent, q_lora=q_lora, kv_lora=kv_lora),
        [(q_lora, q_lora, BF16), (kv_lora, kv_lora, BF16), (LANES, LANES, BF16)],
        extras=[(g_qa.reshape(1, -1), (1, q_lora), lambda i, j: (0, 0)),
                (g_kva.reshape(1, -1), (1, kv_lora), lambda i, j: (0, 0)),
                (ca, (tml, LANES), lat_map), (saa, (tml, LANES), lat_map), (sba, (tml, LANES), lat_map)],
        tm=tml, tn=w_lat.shape[1], name="in_proj_latent")

    (q_a,) = _mm(cqn, w_uq_p, functools.partial(_epi_mla_q, scale=mla_scale),
                 [(w_uq_p.shape[1], _tile(w_uq_p.shape[1], 1024), BF16)],
                 extras=[(ca_s, (tm, LANES), tab_map), (saa_s, (tm, LANES), tab_map),
                         (sba_s, (tm, LANES), tab_map)],
                 tm=tm, tn=1024, name="mla_q_up")
    (kv_a,) = _mm(ckvn, w_ukv_p, _epi_cast, [(w_ukv_p.shape[1], _tile(w_ukv_p.shape[1], 1024), BF16)],
                  tm=tm, tn=1024, name="mla_kv_up")
    o_a = _attention(q_a, (MLA_QK_PAD, 0),
                     [(kv_a, MLA_NOPE, lambda hh: hh), (kpe, LANES, lambda hh: 0)],
                     kv_a, lambda hh: MLA_HEADS + hh,
                     n_batch=n_batch, n_tok=n_tok, n_heads=MLA_HEADS, dv=MLA_V, name="mla_attention")

    o_b = _fnet(z_plain, u_col0, n_batch=n_batch, n_tok=n_tok, gd=gd)

    group = GQA_HEADS // GQA_KV_HEADS
    o_c = _attention(qk, (GQA_HEAD_DIM, 0),
                     [(qk, GQA_HEAD_DIM, lambda hh: GQA_HEADS + hh // group)],
                     z_plain, lambda hh: v_col0 // GQA_HEAD_DIM + hh // group,
                     n_batch=n_batch, n_tok=n_tok, n_heads=GQA_HEADS, dv=GQA_HEAD_DIM, name="gqa_attention")

    merged = _merge(o_a, o_b, o_c, w_branch.astype(BF16), z_plain, gate_col0, d_model)
    (x,) = _mm(merged, w_o.astype(BF16), _epi_residual, [(d_model, _tile(d_model, 512), F32)],
               extras=[(x, (tm, _tile(d_model, 512)), lambda i, j: (i, j))],
               tm=tm, tn=512, name="out_proj")

    h2 = _rmsnorm(x, g_mlp, BF16, "rmsnorm_mlp")
    d_ff = w_up.shape[1]
    (act,) = _mm(h2, w_up.astype(BF16), _epi_relu2, [(d_ff, _tile(d_ff, 1024), BF16)],
                 tm=tm, tn=1024, name="mlp_up")
    (x,) = _mm(act, w_down.astype(BF16), _epi_residual, [(d_model, _tile(d_model, 1024), F32)],
               extras=[(x, (tm, _tile(d_model, 1024)), lambda i, j: (i, j))],
               tm=tm, tn=1024, tk=2048, name="mlp_down")
    return x


def kernel(x_prompt, x_sample, w_in, g_attn, g_qa, w_uq, g_kva, w_ukv, g_qk_q, g_qk_k,
           w_branch, w_o, g_mlp, w_up, w_down, g_final):
    n_tok, d_model = x_prompt.shape[1], x_prompt.shape[2]
    assert x_sample.shape[1:] == (n_tok, d_model)
    b_p, b_s = x_prompt.shape[0], x_sample.shape[0]
    n_batch = b_p + b_s
    x = jnp.concatenate([x_prompt, x_sample], axis=0).reshape(n_batch * n_tok, d_model)

    mla_scale = (MLA_NOPE + MLA_ROPE) ** -0.5
    tabs = (_rope_tables(n_tok, MLA_ROPE, 1.0), _rope_tables(n_tok, MLA_ROPE, mla_scale),
            _rope_tables(n_tok, GQA_HEAD_DIM, 1.0))
    for l in range(w_in.shape[0]):
        x = _layer(x, n_batch, n_tok, tabs, w_in[l], g_attn[l], g_qa[l], w_uq[l], g_kva[l], w_ukv[l],
                   g_qk_q[l], g_qk_k[l], w_branch[l], w_o[l], g_mlp[l], w_up[l], w_down[l])
    y = _rmsnorm(x, g_final, F32, "rmsnorm_final")
    y = y.reshape(n_batch, n_tok, d_model)
    return (y[:b_p], y[b_p:])
```

```python
import functools
import math

import jax
import jax.numpy as jnp
from jax import lax
from jax.experimental import pallas as pl
from jax.experimental.pallas import tpu as pltpu

F32 = jnp.float32
BF16 = jnp.bfloat16

GRID_W = 64
ROPE_THETA = 10000.0
EPS = 1e-6
MLA_HEADS = 16
MLA_NOPE = 128
MLA_ROPE = 64
MLA_V = 128
FNET_GROUPS = 4
GQA_HEADS = 16
GQA_KV_HEADS = 4
GQA_HEAD_DIM = 128
N_BRANCH = 3

LANES = 128
MXU_COLS = 256
MLA_QK_PAD = 2 * LANES
VMEM_LIMIT_BYTES = 56 * 2 ** 20
LOG2E = math.log2(math.e)
ATTN_Q_BLOCK = 2048
MLA_ROW_CHUNK = 256
GQA_ROW_CHUNK = 128


def _tile(n, pref):
    if n <= pref:
        return n
    t = (pref // LANES) * LANES
    while t > LANES and n % t:
        t -= LANES
    assert n % t == 0, (n, pref)
    return t


def _params(*sem):
    return pltpu.CompilerParams(dimension_semantics=sem, vmem_limit_bytes=VMEM_LIMIT_BYTES)


def _rms(x, g):
    return (x * lax.rsqrt(jnp.mean(x * x, axis=-1, keepdims=True) + EPS)) * g


def _rmsnorm_kernel(x_ref, g_ref, o_ref):
    o_ref[...] = _rms(x_ref[...], g_ref[...]).astype(o_ref.dtype)


def _rmsnorm_join_kernel(xa_ref, xb_ref, g_ref, o_ref, *, nb_a):
    i = pl.program_id(0)

    @pl.when(i < nb_a)
    def _():
        o_ref[...] = _rms(xa_ref[...], g_ref[...]).astype(o_ref.dtype)

    @pl.when(i >= nb_a)
    def _():
        o_ref[...] = _rms(xb_ref[...], g_ref[...]).astype(o_ref.dtype)


def _rmsnorm_split_kernel(x_ref, g_ref, oa_ref, ob_ref, *, nb_a):
    i = pl.program_id(0)
    y = _rms(x_ref[...], g_ref[...]).astype(oa_ref.dtype)

    @pl.when(i < nb_a)
    def _():
        oa_ref[...] = y

    @pl.when(i >= nb_a)
    def _():
        ob_ref[...] = y


def _rmsnorm(x, g, out_dtype, name):
    t, d = x.shape
    tm = _tile(t, 256)
    return pl.pallas_call(
        _rmsnorm_kernel,
        grid=(t // tm,),
        in_specs=[pl.BlockSpec((tm, d), lambda i: (i, 0)),
                  pl.BlockSpec((1, d), lambda i: (0, 0))],
        out_specs=pl.BlockSpec((tm, d), lambda i: (i, 0)),
        out_shape=jax.ShapeDtypeStruct((t, d), out_dtype),
        compiler_params=_params("parallel"),
        name=name,
    )(x, g.reshape(1, d))


def _rmsnorm_join(xa, xb, g, out_dtype, name):
    (ta, d), tb = xa.shape, xb.shape[0]
    tm = _tile(math.gcd(ta, tb), 256)
    nb_a = ta // tm
    return pl.pallas_call(
        functools.partial(_rmsnorm_join_kernel, nb_a=nb_a),
        grid=((ta + tb) // tm,),
        in_specs=[pl.BlockSpec((tm, d), lambda i: (jnp.minimum(i, nb_a - 1), 0)),
                  pl.BlockSpec((tm, d), lambda i: (jnp.maximum(i - nb_a, 0), 0)),
                  pl.BlockSpec((1, d), lambda i: (0, 0))],
        out_specs=pl.BlockSpec((tm, d), lambda i: (i, 0)),
        out_shape=jax.ShapeDtypeStruct((ta + tb, d), out_dtype),
        compiler_params=_params("arbitrary"),
        name=name,
    )(xa, xb, g.reshape(1, d))


def _rmsnorm_split(x, g, ta, out_dtype, name):
    t, d = x.shape
    tm = _tile(math.gcd(ta, t - ta), 256)
    nb_a = ta // tm
    return pl.pallas_call(
        functools.partial(_rmsnorm_split_kernel, nb_a=nb_a),
        grid=(t // tm,),
        in_specs=[pl.BlockSpec((tm, d), lambda i: (i, 0)),
                  pl.BlockSpec((1, d), lambda i: (0, 0))],
        out_specs=[pl.BlockSpec((tm, d), lambda i: (jnp.minimum(i, nb_a - 1), 0)),
                   pl.BlockSpec((tm, d), lambda i: (jnp.maximum(i - nb_a, 0), 0))],
        out_shape=[jax.ShapeDtypeStruct((ta, d), out_dtype),
                   jax.ShapeDtypeStruct((t - ta, d), out_dtype)],
        compiler_params=_params("arbitrary"),
        name=name,
    )(x, g.reshape(1, d))


def _mm_kernel(a_ref, w_ref, *refs, epi, n_extra, n_out, nk, chunks):
    extras = refs[:n_extra]
    outs = refs[n_extra:n_extra + n_out]
    if nk == 1:
        for c0, cw in chunks:
            acc = jnp.dot(a_ref[...], w_ref[:, c0:c0 + cw], preferred_element_type=F32)
            epi(acc, c0, extras, outs)
        return
    acc_ref = refs[n_extra + n_out]
    k = pl.program_id(2)

    @pl.when(k == 0)
    def _():
        acc_ref[...] = jnp.zeros_like(acc_ref)

    acc_ref[...] += jnp.dot(a_ref[...], w_ref[...], preferred_element_type=F32)

    @pl.when(k == nk - 1)
    def _():
        epi(acc_ref[...], 0, extras, outs)


def _mm(a, w, epi, outs, extras=(), *, tm, tn, tk=None, chunk=None, chunks=None, layer=None, name):
    m, kdim = a.shape
    n = w.shape[-1]
    tm = _tile(m, tm)
    tn = _tile(n, tn)
    tk = kdim if tk is None else _tile(kdim, tk)
    nk = kdim // tk
    if chunks is None:
        cw = tn if chunk is None else min(chunk, tn)
        chunks = [(c0, cw) for c0 in range(0, tn, cw)]
    assert nk == 1 or len(chunks) == 1
    grid = (m // tm, n // tn, nk)
    if w.ndim == 3:
        w_spec = pl.BlockSpec((None, tk, tn), lambda i, j, k: (layer, k, j))
    else:
        w_spec = pl.BlockSpec((tk, tn), lambda i, j, k: (k, j))
    in_specs = [pl.BlockSpec((tm, tk), lambda i, j, k: (i, k)), w_spec]
    for _, bshape, imap in extras:
        in_specs.append(pl.BlockSpec(bshape, lambda i, j, k, imap=imap: imap(i, j)))
    out_specs = [pl.BlockSpec((tm, bc), lambda i, j, k: (i, j)) for _, bc, _ in outs]
    out_shape = [jax.ShapeDtypeStruct((m, tc), dt) for tc, _, dt in outs]
    scratch = [pltpu.VMEM((tm, tn), F32)] if nk > 1 else []
    return pl.pallas_call(
        functools.partial(_mm_kernel, epi=epi, n_extra=len(extras), n_out=len(outs), nk=nk,
                          chunks=chunks),
        grid=grid,
        in_specs=in_specs,
        out_specs=out_specs,
        out_shape=out_shape,
        scratch_shapes=scratch,
        compiler_params=_params("parallel", "parallel", "arbitrary"),
        name=name,
    )(a, w, *[e[0] for e in extras])


def _cols(c0, acc):
    return slice(c0, c0 + acc.shape[1])


def _epi_cast(acc, c0, extras, outs):
    outs[0][:, _cols(c0, acc)] = acc.astype(outs[0].dtype)


def _epi_relu2(acc, c0, extras, outs):
    r = jnp.maximum(acc, 0.0)
    outs[0][:, _cols(c0, acc)] = (r * r).astype(outs[0].dtype)


def _epi_residual(acc, c0, extras, outs):
    cs = _cols(c0, acc)
    outs[0][:, cs] = extras[0][:, cs] + acc


def _epi_residual_join(acc, c0, extras, outs, *, nb_a):
    cs = _cols(c0, acc)
    i = pl.program_id(0)

    @pl.when(i < nb_a)
    def _():
        outs[0][:, cs] = extras[0][:, cs] + acc

    @pl.when(i >= nb_a)
    def _():
        outs[0][:, cs] = extras[1][:, cs] + acc


def _rope(x, c, sa, sb, half):
    return x * c + pltpu.roll(x, LANES - half, 1) * sa + pltpu.roll(x, half, 1) * sb


def _epi_latent(acc, c0, extras, outs, *, q_lora, kv_lora):
    gq, gkv, c, sa, sb = extras
    if c0 == 0:
        outs[0][...] = _rms(acc, gq[...]).astype(BF16)
    elif c0 == q_lora:
        outs[1][...] = _rms(acc, gkv[...]).astype(BF16)
    else:
        outs[2][...] = _rope(acc, c[...], sa[...], sb[...], MLA_ROPE // 4).astype(BF16)


def _epi_gqa_heads(acc, c0, extras, outs):
    g, c, sa, sb = extras
    cv, sav, sbv = c[...], sa[...], sb[...]
    for h in range(acc.shape[1] // GQA_HEAD_DIM):
        sl = slice(h * GQA_HEAD_DIM, (h + 1) * GQA_HEAD_DIM)
        osl = slice(c0 + h * GQA_HEAD_DIM, c0 + (h + 1) * GQA_HEAD_DIM)
        y = _rms(acc[:, sl], g[:, osl])
        outs[0][:, osl] = _rope(y, cv, sav, sbv, GQA_HEAD_DIM // 4).astype(BF16)


def _epi_mla_q(acc, c0, extras, outs, *, scale):
    c, sa, sb = extras
    cv, sav, sbv = c[...], sa[...], sb[...]
    for h in range(acc.shape[1] // MLA_QK_PAD):
        lo = slice(h * MLA_QK_PAD, h * MLA_QK_PAD + LANES)
        hi = slice(h * MLA_QK_PAD + LANES, (h + 1) * MLA_QK_PAD)
        olo = slice(c0 + lo.start, c0 + lo.stop)
        ohi = slice(c0 + hi.start, c0 + hi.stop)
        outs[0][:, olo] = (acc[:, lo] * scale).astype(BF16)
        outs[0][:, ohi] = _rope(acc[:, hi], cv, sav, sbv, MLA_ROPE // 4).astype(BF16)


def _attn_kernel(q_ref, *refs, has_kpe, rows):
    if has_kpe:
        k1_ref, k2_ref, v_ref, o_ref = refs
        k = jnp.concatenate([k1_ref[...], k2_ref[...]], axis=-1)
    else:
        k_ref, v_ref, o_ref = refs
        k = k_ref[...]
    v = v_ref[...]
    dv = v.shape[1]
    v1 = jnp.concatenate([v, jnp.ones_like(v)], axis=-1)
    for r0 in range(0, q_ref.shape[0], rows):
        q = q_ref[r0:r0 + rows, :]
        s = lax.dot_general(q, k, (((1,), (1,)), ((), ())), preferred_element_type=F32)
        m = jnp.max(s, axis=-1, keepdims=True)
        p = jnp.exp2(s - m).astype(BF16)
        o = jnp.dot(p, v1, preferred_element_type=F32)
        o_ref[r0:r0 + rows, :] = (o[:, :dv] / o[:, dv:]).astype(o_ref.dtype)


def _attention(q, q_spec, ks, v, v_spec, *, n_batch, n_tok, n_heads, dv, row_chunk, name):
    tq = _tile(n_tok, ATTN_Q_BLOCK)
    nq = n_tok // tq
    rows = min(tq, row_chunk)
    q_cols, q_col0 = q_spec
    in_specs = [pl.BlockSpec((tq, q_cols), lambda b, h, i: (b * nq + i, q_col0 + h))]
    args = [q]
    for arr, cols, colfn in ks:
        in_specs.append(pl.BlockSpec((n_tok, cols), lambda b, h, i, colfn=colfn: (b, colfn(h))))
        args.append(arr)
    in_specs.append(pl.BlockSpec((n_tok, dv), lambda b, h, i: (b, v_spec(h))))
    args.append(v)
    return pl.pallas_call(
        functools.partial(_attn_kernel, has_kpe=len(ks) == 2, rows=rows),
        grid=(n_batch, n_heads, nq),
        in_specs=in_specs,
        out_specs=pl.BlockSpec((tq, dv), lambda b, h, i: (b * nq + i, h)),
        out_shape=jax.ShapeDtypeStruct((n_batch * n_tok, n_heads * dv), BF16),
        compiler_params=_params("parallel", "parallel", "parallel"),
        name=name,
    )(*args)


def _fnet_chan_kernel(u_ref, w_ref, yc_ref, ys_ref):
    gd = yc_ref.shape[1]
    y = jnp.dot(u_ref[...], w_ref[...], preferred_element_type=F32)
    yc_ref[...] = y[:, :gd].astype(BF16)
    ys_ref[...] = y[:, gd:].astype(BF16)


def _fnet_pos_kernel(cn_ref, sn_ref, yc_ref, ys_ref, o_ref):
    o = jnp.dot(cn_ref[...], yc_ref[...], preferred_element_type=F32)
    o = o + jnp.dot(sn_ref[...], ys_ref[...], preferred_element_type=F32)
    o_ref[...] = o.astype(o_ref.dtype)


def _dft_tables(n):
    idx = jnp.arange(n, dtype=jnp.int32)
    jk = (idx[:, None] * idx[None, :]) % n
    ang = jk.astype(F32) * (2.0 * math.pi / n)
    return jnp.cos(ang), jnp.sin(ang)


def _fnet_tables(n_tok, gd):
    norm = 1.0 / math.sqrt(n_tok * gd)
    s_chan = 2.0 ** round(math.log2(norm) / 2)
    s_pos = norm / s_chan
    cc, sc = _dft_tables(gd)
    w_chan = (jnp.concatenate([cc, sc], axis=1) * s_chan).astype(BF16)
    cn, sn = _dft_tables(n_tok)
    return w_chan, (cn * s_pos).astype(BF16), (sn * (-s_pos)).astype(BF16)


def _fnet(z, u_col0, tables, *, n_batch, n_tok, gd):
    t = z.shape[0]
    width = FNET_GROUPS * gd
    w_chan, cn, msn = tables
    tm = _tile(t, 1024)
    ublk = u_col0 // gd
    assert u_col0 % gd == 0
    yc, ys = pl.pallas_call(
        _fnet_chan_kernel,
        grid=(t // tm, FNET_GROUPS),
        in_specs=[pl.BlockSpec((tm, gd), lambda i, g: (i, ublk + g)),
                  pl.BlockSpec((gd, 2 * gd), lambda i, g: (0, 0))],
        out_specs=[pl.BlockSpec((tm, gd), lambda i, g: (i, g))] * 2,
        out_shape=[jax.ShapeDtypeStruct((t, width), BF16)] * 2,
        compiler_params=_params("parallel", "parallel"),
        name="fnet_chan",
    )(z, w_chan)

    tp = _tile(n_tok, 1024)
    tn = _tile(width, 1024)
    npb = n_tok // tp
    return pl.pallas_call(
        _fnet_pos_kernel,
        grid=(n_batch, width // tn, npb),
        in_specs=[pl.BlockSpec((tp, n_tok), lambda b, j, i: (i, 0)),
                  pl.BlockSpec((tp, n_tok), lambda b, j, i: (i, 0)),
                  pl.BlockSpec((n_tok, tn), lambda b, j, i: (b, j)),
                  pl.BlockSpec((n_tok, tn), lambda b, j, i: (b, j))],
        out_specs=pl.BlockSpec((tp, tn), lambda b, j, i: (b * npb + i, j)),
        out_shape=jax.ShapeDtypeStruct((t, width), BF16),
        compiler_params=_params("parallel", "parallel", "parallel"),
        name="fnet_pos",
    )(cn, msn, yc, ys)


def _merge_kernel(oa_ref, ob_ref, oc_ref, w_ref, ga_ref, gb_ref, gc_ref, out_ref):
    tn = out_ref.shape[1]
    cw = min(tn, MXU_COLS)
    for c0 in range(0, tn, cw):
        cs = slice(c0, c0 + cw)
        acc = None
        for b, (o_ref, g_ref) in enumerate(((oa_ref, ga_ref), (ob_ref, gb_ref), (oc_ref, gc_ref))):
            gate = 1.0 / (1.0 + jnp.exp(-g_ref[:, cs].astype(F32)))
            c = gate * jnp.dot(o_ref[...], w_ref[b, :, cs], preferred_element_type=F32)
            acc = c if acc is None else acc + c
        out_ref[:, cs] = acc.astype(out_ref.dtype)


def _merge(oa, ob, oc, w_branch, layer, gates, d_model):
    t, bw = oa.shape
    tm = _tile(t, 512)
    tn = _tile(d_model, 512)
    nj = d_model // tn
    o_spec = pl.BlockSpec((tm, bw), lambda i, j: (i, 0))
    g_specs = [pl.BlockSpec((tm, tn), lambda i, j, b=b: (i, b * nj + j)) for b in range(N_BRANCH)]
    return pl.pallas_call(
        _merge_kernel,
        grid=(t // tm, nj),
        in_specs=[o_spec, o_spec, o_spec,
                  pl.BlockSpec((None, N_BRANCH, bw, tn), lambda i, j: (layer, 0, 0, j))] + g_specs,
        out_specs=pl.BlockSpec((tm, tn), lambda i, j: (i, j)),
        out_shape=jax.ShapeDtypeStruct((t, d_model), BF16),
        compiler_params=_params("parallel", "parallel"),
        name="gated_merge",
    )(oa, ob, oc, w_branch, gates, gates, gates)


def _rope_tables(n_tok, rot_dim, scale):
    rows = n_tok // GRID_W
    row_idx = jnp.broadcast_to(jnp.arange(rows)[:, None], (rows, GRID_W)).reshape(-1).astype(F32)
    col_idx = jnp.broadcast_to(jnp.arange(GRID_W)[None, :], (rows, GRID_W)).reshape(-1).astype(F32)
    nq = rot_dim // 4
    freqs = ROPE_THETA ** (-(2.0 * jnp.arange(nq, dtype=F32)) / (rot_dim // 2))
    ang = jnp.stack([row_idx[:, None] * freqs, col_idx[:, None] * freqs], axis=1)
    cos, sin = jnp.cos(ang), jnp.sin(ang)
    zero = jnp.zeros_like(cos[:, 0])
    pad = jnp.zeros((n_tok, LANES - rot_dim), F32)
    c = jnp.concatenate([cos[:, 0], cos[:, 0], cos[:, 1], cos[:, 1], pad], axis=1)
    sa = jnp.concatenate([-sin[:, 0], zero, -sin[:, 1], zero, pad], axis=1)
    sb = jnp.concatenate([zero, sin[:, 0], zero, sin[:, 1], pad], axis=1)
    return c * scale, sa * scale, sb * scale


def _layer_weights(w_in, w_uq, w_ukv, g_qk_q, g_qk_k, q_lora, kv_lora, d_model, fw, gqa_scale):
    seg = [q_lora, kv_lora, MLA_ROPE, fw, GQA_HEADS * GQA_HEAD_DIM,
           GQA_KV_HEADS * GQA_HEAD_DIM, GQA_KV_HEADS * GQA_HEAD_DIM, N_BRANCH * d_model]
    offs = [0]
    for s in seg:
        offs.append(offs[-1] + s)
    cq, ckv, kpe, uf, qc, kc, vc, gl = [w_in[:, offs[i]:offs[i + 1]] for i in range(8)]
    w_gate = gl.astype(BF16)
    w_uv = jnp.concatenate([uf, vc], axis=1).astype(BF16)
    w_qk = jnp.concatenate([qc, kc], axis=1).astype(BF16)
    kpad = jnp.zeros((w_in.shape[0], LANES - MLA_ROPE), w_in.dtype)
    w_lat = jnp.concatenate([cq, ckv, kpe, kpad], axis=1).astype(BF16)
    g_qk = jnp.concatenate([jnp.tile(g_qk_q * gqa_scale, GQA_HEADS),
                            jnp.tile(g_qk_k, GQA_KV_HEADS)]).reshape(1, -1)
    uq = w_uq.reshape(q_lora, MLA_HEADS, MLA_NOPE + MLA_ROPE)
    uq = jnp.pad(uq, ((0, 0), (0, 0), (0, MLA_QK_PAD - MLA_NOPE - MLA_ROPE)))
    w_uq_p = uq.reshape(q_lora, MLA_HEADS * MLA_QK_PAD).astype(BF16)
    ukv = w_ukv.reshape(kv_lora, MLA_HEADS, MLA_NOPE + MLA_V)
    w_ukv_p = jnp.concatenate([ukv[:, :, :MLA_NOPE].reshape(kv_lora, -1),
                               ukv[:, :, MLA_NOPE:].reshape(kv_lora, -1)], axis=1).astype(BF16)
    return w_gate, w_uv, w_qk, w_lat, g_qk, w_uq_p, w_ukv_p


def _layer(x, layer, n_batch, n_tok, tabs, fnet_tabs, w_in, g_attn, g_qa, w_uq, g_kva, w_ukv,
           g_qk_q, g_qk_k, wb_bf, wo_bf, g_mlp, wup_bf, wdown_bf):
    joined = isinstance(x, tuple)
    t = sum(a.shape[0] for a in x) if joined else x.shape[0]
    d_model = g_attn.shape[0]
    q_lora, kv_lora = g_qa.shape[0], g_kva.shape[0]
    fw = wb_bf.shape[2]
    gd = fw // FNET_GROUPS
    mla_scale = (MLA_NOPE + MLA_ROPE) ** -0.5 * LOG2E
    gqa_scale = GQA_HEAD_DIM ** -0.5 * LOG2E
    (ca, saa, sba), (ca_s, saa_s, sba_s), (cc, sac, sbc) = tabs
    w_gate, w_uv, w_qk, w_lat, g_qk, w_uq_p, w_ukv_p = _layer_weights(
        w_in, w_uq, w_ukv, g_qk_q, g_qk_k, q_lora, kv_lora, d_model, fw, gqa_scale)

    tm = _tile(n_tok, 1024)
    nrb = n_tok // tm
    tab_map = lambda i, j: (i % nrb, 0)

    if joined:
        h = _rmsnorm_join(x[0], x[1], g_attn, BF16, "rmsnorm_attn")
    else:
        h = _rmsnorm(x, g_attn, BF16, "rmsnorm_attn")

    (gates,) = _mm(h, w_gate, _epi_cast, [(w_gate.shape[1], _tile(w_gate.shape[1], 1024), BF16)],
                   tm=tm, tn=1024, name="in_proj_gates")
    (uv,) = _mm(h, w_uv, _epi_cast, [(w_uv.shape[1], _tile(w_uv.shape[1], 1280), BF16)],
                tm=tm, tn=1280, name="in_proj_uv")
    qk_tn = _tile(w_qk.shape[1], 512)
    (qk,) = _mm(h, w_qk, _epi_gqa_heads, [(w_qk.shape[1], qk_tn, BF16)],
                extras=[(g_qk, (1, qk_tn), lambda i, j: (0, j)),
                        (cc, (tm, LANES), tab_map), (sac, (tm, LANES), tab_map), (sbc, (tm, LANES), tab_map)],
                tm=tm, tn=512, chunk=MXU_COLS, name="in_proj_gqa_qk")
    tml = _tile(n_tok, 512)
    nrl = n_tok // tml
    lat_map = lambda i, j: (i % nrl, 0)
    cqn, ckvn, kpe = _mm(
        h, w_lat, functools.partial(_epi_latent, q_lora=q_lora, kv_lora=kv_lora),
        [(q_lora, q_lora, BF16), (kv_lora, kv_lora, BF16), (LANES, LANES, BF16)],
        extras=[(g_qa.reshape(1, -1), (1, q_lora), lambda i, j: (0, 0)),
                (g_kva.reshape(1, -1), (1, kv_lora), lambda i, j: (0, 0)),
                (ca, (tml, LANES), lat_map), (saa, (tml, LANES), lat_map), (sba, (tml, LANES), lat_map)],
        tm=tml, tn=w_lat.shape[1],
        chunks=[(0, q_lora), (q_lora, kv_lora), (q_lora + kv_lora, LANES)], name="in_proj_latent")

    (q_a,) = _mm(cqn, w_uq_p, functools.partial(_epi_mla_q, scale=mla_scale),
                 [(w_uq_p.shape[1], _tile(w_uq_p.shape[1], 1024), BF16)],
                 extras=[(ca_s, (tm, LANES), tab_map), (saa_s, (tm, LANES), tab_map),
                         (sba_s, (tm, LANES), tab_map)],
                 tm=tm, tn=1024, chunk=MLA_QK_PAD, name="mla_q_up")
    (kv_a,) = _mm(ckvn, w_ukv_p, _epi_cast, [(w_ukv_p.shape[1], _tile(w_ukv_p.shape[1], 1024), BF16)],
                  tm=tm, tn=1024, name="mla_kv_up")
    o_a = _attention(q_a, (MLA_QK_PAD, 0),
                     [(kv_a, MLA_NOPE, lambda hh: hh), (kpe, LANES, lambda hh: 0)],
                     kv_a, lambda hh: MLA_HEADS + hh,
                     n_batch=n_batch, n_tok=n_tok, n_heads=MLA_HEADS, dv=MLA_V, row_chunk=MLA_ROW_CHUNK,
                     name="mla_attention")

    o_b = _fnet(uv, 0, fnet_tabs, n_batch=n_batch, n_tok=n_tok, gd=gd)

    group = GQA_HEADS // GQA_KV_HEADS
    o_c = _attention(qk, (GQA_HEAD_DIM, 0),
                     [(qk, GQA_HEAD_DIM, lambda hh: GQA_HEADS + hh // group)],
                     uv, lambda hh: fw // GQA_HEAD_DIM + hh // group,
                     n_batch=n_batch, n_tok=n_tok, n_heads=GQA_HEADS, dv=GQA_HEAD_DIM, row_chunk=GQA_ROW_CHUNK,
                     name="gqa_attention")

    merged = _merge(o_a, o_b, o_c, wb_bf, layer, gates, d_model)
    otn = _tile(d_model, 512)
    if joined:
        nb_a = x[0].shape[0] // tm
        res_extras = [(x[0], (tm, otn), lambda i, j: (jnp.minimum(i, nb_a - 1), j)),
                      (x[1], (tm, otn), lambda i, j: (jnp.maximum(i - nb_a, 0), j))]
        res_epi = functools.partial(_epi_residual_join, nb_a=nb_a)
    else:
        res_extras = [(x, (tm, otn), lambda i, j: (i, j))]
        res_epi = _epi_residual
    (x,) = _mm(merged, wo_bf, res_epi, [(d_model, otn, F32)], extras=res_extras,
               tm=tm, tn=512, layer=layer, name="out_proj")

    h2 = _rmsnorm(x, g_mlp, BF16, "rmsnorm_mlp")
    d_ff = wup_bf.shape[2]
    (act,) = _mm(h2, wup_bf, _epi_relu2, [(d_ff, _tile(d_ff, 1024), BF16)],
                 tm=tm, tn=1024, layer=layer, name="mlp_up")
    (x,) = _mm(act, wdown_bf, _epi_residual, [(d_model, _tile(d_model, 1024), F32)],
               extras=[(x, (tm, _tile(d_model, 1024)), lambda i, j: (i, j))],
               tm=tm, tn=1024, tk=2048, layer=layer, name="mlp_down")
    return x


def kernel(x_prompt, x_sample, w_in, g_attn, g_qa, w_uq, g_kva, w_ukv, g_qk_q, g_qk_k,
           w_branch, w_o, g_mlp, w_up, w_down, g_final):
    n_tok, d_model = x_prompt.shape[1], x_prompt.shape[2]
    assert x_sample.shape[1:] == (n_tok, d_model)
    b_p, b_s = x_prompt.shape[0], x_sample.shape[0]
    n_batch = b_p + b_s
    x = (x_prompt.reshape(b_p * n_tok, d_model), x_sample.reshape(b_s * n_tok, d_model))

    mla_scale = (MLA_NOPE + MLA_ROPE) ** -0.5 * LOG2E
    tabs = (_rope_tables(n_tok, MLA_ROPE, 1.0), _rope_tables(n_tok, MLA_ROPE, mla_scale),
            _rope_tables(n_tok, GQA_HEAD_DIM, 1.0))
    fnet_tabs = _fnet_tables(n_tok, w_branch.shape[2] // FNET_GROUPS)
    wb_bf, wo_bf = w_branch.astype(BF16), w_o.astype(BF16)
    wup_bf, wdown_bf = w_up.astype(BF16), w_down.astype(BF16)
    for l in range(w_in.shape[0]):
        x = _layer(x, l, n_batch, n_tok, tabs, fnet_tabs, w_in[l], g_attn[l], g_qa[l], w_uq[l], g_kva[l],
                   w_ukv[l], g_qk_q[l], g_qk_k[l], wb_bf, wo_bf, g_mlp[l], wup_bf, wdown_bf)
    y_p, y_s = _rmsnorm_split(x, g_final, b_p * n_tok, F32, "rmsnorm_final")
    return (y_p.reshape(b_p, n_tok, d_model), y_s.reshape(b_s, n_tok, d_model))
```

```python
import functools
import math

import jax
import jax.numpy as jnp
from jax import lax
from jax.experimental import pallas as pl
from jax.experimental.pallas import tpu as pltpu

F32 = jnp.float32
BF16 = jnp.bfloat16

GRID_W = 64
ROPE_THETA = 10000.0
EPS = 1e-6
MLA_HEADS = 16
MLA_NOPE = 128
MLA_ROPE = 64
MLA_V = 128
FNET_GROUPS = 4
GQA_HEADS = 16
GQA_KV_HEADS = 4
GQA_HEAD_DIM = 128
N_BRANCH = 3

LANES = 128
MXU_COLS = 256
MLA_QK_PAD = 2 * LANES
VMEM_LIMIT_BYTES = 60 * 2 ** 20
LOG2E = math.log2(math.e)
ATTN_Q_BLOCK = 2048
MLA_ROW_CHUNK = 256
GQA_ROW_CHUNK = 128


def _tile(n, pref):
    if n <= pref:
        return n
    t = (pref // LANES) * LANES
    while t > LANES and n % t:
        t -= LANES
    assert n % t == 0, (n, pref)
    return t


def _params(*sem):
    return pltpu.CompilerParams(dimension_semantics=sem, vmem_limit_bytes=VMEM_LIMIT_BYTES)


def _rms(x, g):
    return (x * lax.rsqrt(jnp.mean(x * x, axis=-1, keepdims=True) + EPS)) * g


def _rmsnorm_kernel(x_ref, g_ref, o_ref):
    o_ref[...] = _rms(x_ref[...], g_ref[...]).astype(o_ref.dtype)


def _rmsnorm_join_kernel(xa_ref, xb_ref, g_ref, o_ref, *, nb_a):
    i = pl.program_id(0)

    @pl.when(i < nb_a)
    def _():
        o_ref[...] = _rms(xa_ref[...], g_ref[...]).astype(o_ref.dtype)

    @pl.when(i >= nb_a)
    def _():
        o_ref[...] = _rms(xb_ref[...], g_ref[...]).astype(o_ref.dtype)


def _rmsnorm_split_kernel(x_ref, g_ref, oa_ref, ob_ref, *, nb_a):
    i = pl.program_id(0)
    y = _rms(x_ref[...], g_ref[...]).astype(oa_ref.dtype)

    @pl.when(i < nb_a)
    def _():
        oa_ref[...] = y

    @pl.when(i >= nb_a)
    def _():
        ob_ref[...] = y


def _rmsnorm(x, g, out_dtype, name):
    t, d = x.shape
    tm = _tile(t, 256)
    return pl.pallas_call(
        _rmsnorm_kernel,
        grid=(t // tm,),
        in_specs=[pl.BlockSpec((tm, d), lambda i: (i, 0)),
                  pl.BlockSpec((1, d), lambda i: (0, 0))],
        out_specs=pl.BlockSpec((tm, d), lambda i: (i, 0)),
        out_shape=jax.ShapeDtypeStruct((t, d), out_dtype),
        compiler_params=_params("parallel"),
        name=name,
    )(x, g.reshape(1, d))


def _rmsnorm_join(xa, xb, g, out_dtype, name):
    (ta, d), tb = xa.shape, xb.shape[0]
    tm = _tile(math.gcd(ta, tb), 256)
    nb_a = ta // tm
    return pl.pallas_call(
        functools.partial(_rmsnorm_join_kernel, nb_a=nb_a),
        grid=((ta + tb) // tm,),
        in_specs=[pl.BlockSpec((tm, d), lambda i: (jnp.minimum(i, nb_a - 1), 0)),
                  pl.BlockSpec((tm, d), lambda i: (jnp.maximum(i - nb_a, 0), 0)),
                  pl.BlockSpec((1, d), lambda i: (0, 0))],
        out_specs=pl.BlockSpec((tm, d), lambda i: (i, 0)),
        out_shape=jax.ShapeDtypeStruct((ta + tb, d), out_dtype),
        compiler_params=_params("arbitrary"),
        name=name,
    )(xa, xb, g.reshape(1, d))


def _rmsnorm_split(x, g, ta, out_dtype, name):
    t, d = x.shape
    tm = _tile(math.gcd(ta, t - ta), 256)
    nb_a = ta // tm
    return pl.pallas_call(
        functools.partial(_rmsnorm_split_kernel, nb_a=nb_a),
        grid=(t // tm,),
        in_specs=[pl.BlockSpec((tm, d), lambda i: (i, 0)),
                  pl.BlockSpec((1, d), lambda i: (0, 0))],
        out_specs=[pl.BlockSpec((tm, d), lambda i: (jnp.minimum(i, nb_a - 1), 0)),
                   pl.BlockSpec((tm, d), lambda i: (jnp.maximum(i - nb_a, 0), 0))],
        out_shape=[jax.ShapeDtypeStruct((ta, d), out_dtype),
                   jax.ShapeDtypeStruct((t - ta, d), out_dtype)],
        compiler_params=_params("arbitrary"),
        name=name,
    )(x, g.reshape(1, d))


def _mm_kernel(a_ref, w_ref, *refs, epi, n_extra, n_out, nk, chunks):
    extras = refs[:n_extra]
    outs = refs[n_extra:n_extra + n_out]
    if nk == 1:
        for c0, cw in chunks:
            acc = jnp.dot(a_ref[...], w_ref[:, c0:c0 + cw], preferred_element_type=F32)
            epi(acc, c0, extras, outs)
        return
    assert epi is _epi_residual
    k = pl.program_id(2)
    d = jnp.dot(a_ref[...], w_ref[...], preferred_element_type=F32)

    @pl.when(k == 0)
    def _():
        outs[0][...] = extras[0][...] + d

    @pl.when(k > 0)
    def _():
        outs[0][...] += d


def _mm(a, w, epi, outs, extras=(), *, tm, tn, tk=None, chunk=None, chunks=None, layer=None,
        resident_w=False, name):
    m, kdim = a.shape
    n = w.shape[-1]
    tm = _tile(m, tm)
    tn = _tile(n, tn)
    tk = kdim if tk is None else _tile(kdim, tk)
    nk = kdim // tk
    if chunks is None:
        cw = tn if chunk is None else min(chunk, tn)
        chunks = [(c0, cw) for c0 in range(0, tn, cw)]
    assert nk == 1 or len(chunks) == 1
    grid = (m // tm, n // tn, nk)
    w_mode = dict(pipeline_mode=pl.Buffered(1)) if resident_w else {}
    assert not resident_w or (n == tn and nk == 1)
    if w.ndim == 3:
        w_spec = pl.BlockSpec((None, tk, tn), lambda i, j, k: (layer, k, j), **w_mode)
    else:
        w_spec = pl.BlockSpec((tk, tn), lambda i, j, k: (k, j), **w_mode)
    in_specs = [pl.BlockSpec((tm, tk), lambda i, j, k: (i, k)), w_spec]
    for _, bshape, imap in extras:
        in_specs.append(pl.BlockSpec(bshape, lambda i, j, k, imap=imap: imap(i, j)))
    out_specs = [pl.BlockSpec((tm, bc), lambda i, j, k: (i, j)) for _, bc, _ in outs]
    out_shape = [jax.ShapeDtypeStruct((m, tc), dt) for tc, _, dt in outs]
    return pl.pallas_call(
        functools.partial(_mm_kernel, epi=epi, n_extra=len(extras), n_out=len(outs), nk=nk,
                          chunks=chunks),
        grid=grid,
        in_specs=in_specs,
        out_specs=out_specs,
        out_shape=out_shape,
        compiler_params=_params("parallel", "parallel", "arbitrary"),
        name=name,
    )(a, w, *[e[0] for e in extras])


def _cols(c0, acc):
    return slice(c0, c0 + acc.shape[1])


def _epi_cast(acc, c0, extras, outs):
    outs[0][:, _cols(c0, acc)] = acc.astype(outs[0].dtype)


def _epi_relu2(acc, c0, extras, outs):
    r = jnp.maximum(acc, 0.0)
    outs[0][:, _cols(c0, acc)] = (r * r).astype(outs[0].dtype)


def _epi_residual(acc, c0, extras, outs):
    cs = _cols(c0, acc)
    outs[0][:, cs] = extras[0][:, cs] + acc


def _epi_residual_join(acc, c0, extras, outs, *, nb_a):
    cs = _cols(c0, acc)
    i = pl.program_id(0)

    @pl.when(i < nb_a)
    def _():
        outs[0][:, cs] = extras[0][:, cs] + acc

    @pl.when(i >= nb_a)
    def _():
        outs[0][:, cs] = extras[1][:, cs] + acc


def _rope(x, c, s):
    return x * c + pltpu.roll(x, LANES // 2, 1) * s


def _epi_latent(acc, c0, extras, outs, *, q_lora, kv_lora):
    gq, gkv, c, s = extras
    if c0 == 0:
        outs[0][...] = _rms(acc, gq[...]).astype(BF16)
    elif c0 == q_lora:
        outs[1][...] = _rms(acc, gkv[...]).astype(BF16)
    else:
        outs[2][...] = _rope(acc, c[...], s[...]).astype(BF16)


def _epi_gqa_heads(acc, c0, extras, outs):
    g, c, s, ones_bd = extras
    cv, sv = c[...], s[...]
    for g0 in range(0, acc.shape[1], MXU_COLS):
        x = acc[:, g0:g0 + MXU_COLS]
        x2 = x * x
        hi = x2.astype(BF16)
        lo = (x2 - hi.astype(F32)).astype(BF16)
        ss = jnp.dot(jnp.concatenate([hi, lo], axis=1), ones_bd[...], preferred_element_type=F32)
        y = (x * lax.rsqrt(ss * (1.0 / GQA_HEAD_DIM) + EPS)) * g[:, c0 + g0:c0 + g0 + MXU_COLS]
        for h0 in range(0, MXU_COLS, GQA_HEAD_DIM):
            osl = slice(c0 + g0 + h0, c0 + g0 + h0 + GQA_HEAD_DIM)
            outs[0][:, osl] = _rope(y[:, h0:h0 + GQA_HEAD_DIM], cv, sv).astype(BF16)


def _epi_mla_q(acc, c0, extras, outs, *, scale):
    c, s = extras
    cv, sv = c[...], s[...]
    for h in range(acc.shape[1] // MLA_QK_PAD):
        lo = slice(h * MLA_QK_PAD, h * MLA_QK_PAD + LANES)
        hi = slice(h * MLA_QK_PAD + LANES, (h + 1) * MLA_QK_PAD)
        olo = slice(c0 + lo.start, c0 + lo.stop)
        ohi = slice(c0 + hi.start, c0 + hi.stop)
        outs[0][:, olo] = (acc[:, lo] * scale).astype(BF16)
        outs[0][:, ohi] = _rope(acc[:, hi], cv, sv).astype(BF16)


def _attn_kernel(q_ref, *refs, has_kpe, rows):
    if has_kpe:
        k1_ref, k2_ref, v_ref, o_ref = refs
        k = jnp.concatenate([k1_ref[...], k2_ref[...]], axis=-1)
    else:
        k_ref, v_ref, o_ref = refs
        k = k_ref[...]
    v = v_ref[...]
    dv = v.shape[1]
    v1 = jnp.concatenate([v, jnp.ones_like(v)], axis=-1)
    for r0 in range(0, q_ref.shape[0], rows):
        q = q_ref[r0:r0 + rows, :]
        s = lax.dot_general(q, k, (((1,), (1,)), ((), ())), preferred_element_type=F32)
        m = jnp.max(s, axis=-1, keepdims=True)
        p = jnp.exp2(s - m).astype(BF16)
        o = jnp.dot(p, v1, preferred_element_type=F32)
        o_ref[r0:r0 + rows, :] = (o[:, :dv] / o[:, dv:]).astype(o_ref.dtype)


def _attention(q, q_spec, ks, v, v_spec, *, n_batch, n_tok, n_heads, dv, row_chunk, name):
    tq = _tile(n_tok, ATTN_Q_BLOCK)
    nq = n_tok // tq
    rows = min(tq, row_chunk)
    q_cols, q_col0 = q_spec
    in_specs = [pl.BlockSpec((tq, q_cols), lambda b, h, i: (b * nq + i, q_col0 + h))]
    args = [q]
    for arr, cols, colfn in ks:
        in_specs.append(pl.BlockSpec((n_tok, cols), lambda b, h, i, colfn=colfn: (b, colfn(h))))
        args.append(arr)
    in_specs.append(pl.BlockSpec((n_tok, dv), lambda b, h, i: (b, v_spec(h))))
    args.append(v)
    return pl.pallas_call(
        functools.partial(_attn_kernel, has_kpe=len(ks) == 2, rows=rows),
        grid=(n_batch, n_heads, nq),
        in_specs=in_specs,
        out_specs=pl.BlockSpec((tq, dv), lambda b, h, i: (b * nq + i, h)),
        out_shape=jax.ShapeDtypeStruct((n_batch * n_tok, n_heads * dv), BF16),
        compiler_params=_params("parallel", "parallel", "parallel"),
        name=name,
    )(*args)


def _fnet_chan_kernel(u_ref, w_ref, yc_ref, ys_ref):
    gd = yc_ref.shape[1]
    y = jnp.dot(u_ref[...], w_ref[...], preferred_element_type=F32)
    yc_ref[...] = y[:, :gd].astype(BF16)
    ys_ref[...] = y[:, gd:].astype(BF16)


def _fnet_pos_kernel(cn_ref, sn_ref, yc_ref, ys_ref, o_ref):
    o = jnp.dot(cn_ref[...], yc_ref[...], preferred_element_type=F32)
    o = o + jnp.dot(sn_ref[...], ys_ref[...], preferred_element_type=F32)
    o_ref[...] = o.astype(o_ref.dtype)


def _dft_tables(n):
    idx = jnp.arange(n, dtype=jnp.int32)
    jk = (idx[:, None] * idx[None, :]) % n
    ang = jk.astype(F32) * (2.0 * math.pi / n)
    return jnp.cos(ang), jnp.sin(ang)


def _fnet_tables(n_tok, gd):
    norm = 1.0 / math.sqrt(n_tok * gd)
    s_chan = 2.0 ** round(math.log2(norm) / 2)
    s_pos = norm / s_chan
    cc, sc = _dft_tables(gd)
    w_chan = (jnp.concatenate([cc, sc], axis=1) * s_chan).astype(BF16)
    cn, sn = _dft_tables(n_tok)
    return w_chan, (cn * s_pos).astype(BF16), (sn * (-s_pos)).astype(BF16)


def _fnet(z, u_col0, tables, *, n_batch, n_tok, gd):
    t = z.shape[0]
    width = FNET_GROUPS * gd
    w_chan, cn, msn = tables
    tm = _tile(t, 1024)
    ublk = u_col0 // gd
    assert u_col0 % gd == 0
    yc, ys = pl.pallas_call(
        _fnet_chan_kernel,
        grid=(t // tm, FNET_GROUPS),
        in_specs=[pl.BlockSpec((tm, gd), lambda i, g: (i, ublk + g)),
                  pl.BlockSpec((gd, 2 * gd), lambda i, g: (0, 0))],
        out_specs=[pl.BlockSpec((tm, gd), lambda i, g: (i, g))] * 2,
        out_shape=[jax.ShapeDtypeStruct((t, width), BF16)] * 2,
        compiler_params=_params("parallel", "parallel"),
        name="fnet_chan",
    )(z, w_chan)

    tp = _tile(n_tok, 1024)
    tn = _tile(width, 1024)
    npb = n_tok // tp
    return pl.pallas_call(
        _fnet_pos_kernel,
        grid=(n_batch, width // tn, npb),
        in_specs=[pl.BlockSpec((tp, n_tok), lambda b, j, i: (i, 0)),
                  pl.BlockSpec((tp, n_tok), lambda b, j, i: (i, 0)),
                  pl.BlockSpec((n_tok, tn), lambda b, j, i: (b, j)),
                  pl.BlockSpec((n_tok, tn), lambda b, j, i: (b, j))],
        out_specs=pl.BlockSpec((tp, tn), lambda b, j, i: (b * npb + i, j)),
        out_shape=jax.ShapeDtypeStruct((t, width), BF16),
        compiler_params=_params("parallel", "parallel", "parallel"),
        name="fnet_pos",
    )(cn, msn, yc, ys)


def _merge_kernel(oa_ref, ob_ref, oc_ref, w_ref, ga_ref, gb_ref, gc_ref, out_ref):
    tn = out_ref.shape[1]
    cw = min(tn, MXU_COLS)
    for c0 in range(0, tn, cw):
        cs = slice(c0, c0 + cw)
        acc = None
        for b, (o_ref, g_ref) in enumerate(((oa_ref, ga_ref), (ob_ref, gb_ref), (oc_ref, gc_ref))):
            gate = 1.0 / (1.0 + jnp.exp(-g_ref[:, cs].astype(F32)))
            c = gate * jnp.dot(o_ref[...], w_ref[b, :, cs], preferred_element_type=F32)
            acc = c if acc is None else acc + c
        out_ref[:, cs] = acc.astype(out_ref.dtype)


def _merge(oa, ob, oc, w_branch, layer, gates, d_model):
    t, bw = oa.shape
    tm = _tile(t, 512)
    tn = _tile(d_model, 512)
    nj = d_model // tn
    o_spec = pl.BlockSpec((tm, bw), lambda i, j: (i, 0))
    g_specs = [pl.BlockSpec((tm, tn), lambda i, j, b=b: (i, b * nj + j)) for b in range(N_BRANCH)]
    return pl.pallas_call(
        _merge_kernel,
        grid=(t // tm, nj),
        in_specs=[o_spec, o_spec, o_spec,
                  pl.BlockSpec((None, N_BRANCH, bw, tn), lambda i, j: (layer, 0, 0, j))] + g_specs,
        out_specs=pl.BlockSpec((tm, tn), lambda i, j: (i, j)),
        out_shape=jax.ShapeDtypeStruct((t, d_model), BF16),
        compiler_params=_params("parallel", "parallel"),
        name="gated_merge",
    )(oa, ob, oc, w_branch, gates, gates, gates)


def _rope_tables(n_tok, rot_dim, scale):
    rows = n_tok // GRID_W
    row_idx = jnp.broadcast_to(jnp.arange(rows)[:, None], (rows, GRID_W)).reshape(-1).astype(F32)
    col_idx = jnp.broadcast_to(jnp.arange(GRID_W)[None, :], (rows, GRID_W)).reshape(-1).astype(F32)
    nq = rot_dim // 4
    freqs = ROPE_THETA ** (-(2.0 * jnp.arange(nq, dtype=F32)) / (rot_dim // 2))
    ang = jnp.stack([row_idx[:, None] * freqs, col_idx[:, None] * freqs], axis=1)
    cos, sin = jnp.cos(ang), jnp.sin(ang)
    pad = jnp.zeros((n_tok, LANES // 2 - 2 * nq), F32)
    cblk = jnp.concatenate([cos[:, 0], cos[:, 1], pad], axis=1)
    sblk = jnp.concatenate([sin[:, 0], sin[:, 1], pad], axis=1)
    c = jnp.concatenate([cblk, cblk], axis=1)
    s = jnp.concatenate([-sblk, sblk], axis=1)
    return c * scale, s * scale


def _rotary_lanes(w, rot_dim):
    nq = rot_dim // 4
    lead = w.shape[:-1]
    r = jnp.swapaxes(w.reshape(*lead, 2, 2, nq), -3, -2).reshape(*lead, 2, 2 * nq)
    r = jnp.pad(r, [(0, 0)] * (len(lead) + 1) + [(0, LANES // 2 - 2 * nq)])
    return r.reshape(*lead, LANES)


def _layer_weights(w_in, w_uq, w_ukv, g_qk_q, g_qk_k, q_lora, kv_lora, d_model, fw, gqa_scale):
    seg = [q_lora, kv_lora, MLA_ROPE, fw, GQA_HEADS * GQA_HEAD_DIM,
           GQA_KV_HEADS * GQA_HEAD_DIM, GQA_KV_HEADS * GQA_HEAD_DIM, N_BRANCH * d_model]
    offs = [0]
    for s in seg:
        offs.append(offs[-1] + s)
    cq, ckv, kpe, uf, qc, kc, vc, gl = [w_in[:, offs[i]:offs[i + 1]] for i in range(8)]
    w_gate = gl.astype(BF16)
    w_uv = jnp.concatenate([uf, vc], axis=1).astype(BF16)
    d_in = w_in.shape[0]
    qc = _rotary_lanes(qc.reshape(d_in, GQA_HEADS, GQA_HEAD_DIM), GQA_HEAD_DIM).reshape(d_in, -1)
    kc = _rotary_lanes(kc.reshape(d_in, GQA_KV_HEADS, GQA_HEAD_DIM), GQA_HEAD_DIM).reshape(d_in, -1)
    w_qk = jnp.concatenate([qc, kc], axis=1).astype(BF16)
    w_lat = jnp.concatenate([cq, ckv, _rotary_lanes(kpe, MLA_ROPE)], axis=1).astype(BF16)
    g_qk = jnp.concatenate([jnp.tile(_rotary_lanes(g_qk_q, GQA_HEAD_DIM) * gqa_scale, GQA_HEADS),
                            jnp.tile(_rotary_lanes(g_qk_k, GQA_HEAD_DIM), GQA_KV_HEADS)]).reshape(1, -1)
    uq = w_uq.reshape(q_lora, MLA_HEADS, MLA_NOPE + MLA_ROPE)
    uq = jnp.concatenate([uq[:, :, :MLA_NOPE], _rotary_lanes(uq[:, :, MLA_NOPE:], MLA_ROPE)], axis=2)
    w_uq_p = uq.reshape(q_lora, MLA_HEADS * MLA_QK_PAD).astype(BF16)
    ukv = w_ukv.reshape(kv_lora, MLA_HEADS, MLA_NOPE + MLA_V)
    w_ukv_p = jnp.concatenate([ukv[:, :, :MLA_NOPE].reshape(kv_lora, -1),
                               ukv[:, :, MLA_NOPE:].reshape(kv_lora, -1)], axis=1).astype(BF16)
    return w_gate, w_uv, w_qk, w_lat, g_qk, w_uq_p, w_ukv_p


def _layer(x, layer, n_batch, n_tok, tabs, fnet_tabs, w_in, g_attn, g_qa, w_uq, g_kva, w_ukv,
           g_qk_q, g_qk_k, wb_bf, wo_bf, g_mlp, wup_bf, wdown_bf):
    joined = isinstance(x, tuple)
    t = sum(a.shape[0] for a in x) if joined else x.shape[0]
    d_model = g_attn.shape[0]
    q_lora, kv_lora = g_qa.shape[0], g_kva.shape[0]
    fw = wb_bf.shape[2]
    gd = fw // FNET_GROUPS
    mla_scale = (MLA_NOPE + MLA_ROPE) ** -0.5 * LOG2E
    gqa_scale = GQA_HEAD_DIM ** -0.5 * LOG2E
    (ca, sa), (ca_s, sa_s), (cc, sc), ones_bd = tabs
    w_gate, w_uv, w_qk, w_lat, g_qk, w_uq_p, w_ukv_p = _layer_weights(
        w_in, w_uq, w_ukv, g_qk_q, g_qk_k, q_lora, kv_lora, d_model, fw, gqa_scale)

    tm = _tile(n_tok, 1024)
    nrb = n_tok // tm
    tab_map = lambda i, j: (i % nrb, 0)

    if joined:
        h = _rmsnorm_join(x[0], x[1], g_attn, BF16, "rmsnorm_attn")
    else:
        h = _rmsnorm(x, g_attn, BF16, "rmsnorm_attn")

    (gates,) = _mm(h, w_gate, _epi_cast, [(w_gate.shape[1], _tile(w_gate.shape[1], 1024), BF16)],
                   tm=tm, tn=1024, name="in_proj_gates")
    (uv,) = _mm(h, w_uv, _epi_cast, [(w_uv.shape[1], _tile(w_uv.shape[1], 1280), BF16)],
                tm=tm, tn=1280, name="in_proj_uv")
    qk_tn = _tile(w_qk.shape[1], 512)
    (qk,) = _mm(h, w_qk, _epi_gqa_heads, [(w_qk.shape[1], qk_tn, BF16)],
                extras=[(g_qk, (1, qk_tn), lambda i, j: (0, j)),
                        (cc, (tm, LANES), tab_map), (sc, (tm, LANES), tab_map),
                        (ones_bd, ones_bd.shape, lambda i, j: (0, 0))],
                tm=tm, tn=512, name="in_proj_gqa_qk")
    cqn, ckvn, kpe = _mm(
        h, w_lat, functools.partial(_epi_latent, q_lora=q_lora, kv_lora=kv_lora),
        [(q_lora, q_lora, BF16), (kv_lora, kv_lora, BF16), (LANES, LANES, BF16)],
        extras=[(g_qa.reshape(1, -1), (1, q_lora), lambda i, j: (0, 0)),
                (g_kva.reshape(1, -1), (1, kv_lora), lambda i, j: (0, 0)),
                (ca, (tm, LANES), tab_map), (sa, (tm, LANES), tab_map)],
        tm=tm, tn=w_lat.shape[1], resident_w=True,
        chunks=[(0, q_lora), (q_lora, kv_lora), (q_lora + kv_lora, LANES)], name="in_proj_latent")

    (q_a,) = _mm(cqn, w_uq_p, functools.partial(_epi_mla_q, scale=mla_scale),
                 [(w_uq_p.shape[1], _tile(w_uq_p.shape[1], 1024), BF16)],
                 extras=[(ca_s, (tm, LANES), tab_map), (sa_s, (tm, LANES), tab_map)],
                 tm=tm, tn=1024, chunk=MLA_QK_PAD, name="mla_q_up")
    (kv_a,) = _mm(ckvn, w_ukv_p, _epi_cast, [(w_ukv_p.shape[1], _tile(w_ukv_p.shape[1], 1024), BF16)],
                  tm=tm, tn=1024, name="mla_kv_up")
    o_a = _attention(q_a, (MLA_QK_PAD, 0),
                     [(kv_a, MLA_NOPE, lambda hh: hh), (kpe, LANES, lambda hh: 0)],
                     kv_a, lambda hh: MLA_HEADS + hh,
                     n_batch=n_batch, n_tok=n_tok, n_heads=MLA_HEADS, dv=MLA_V, row_chunk=MLA_ROW_CHUNK,
                     name="mla_attention")

    o_b = _fnet(uv, 0, fnet_tabs, n_batch=n_batch, n_tok=n_tok, gd=gd)

    group = GQA_HEADS // GQA_KV_HEADS
    o_c = _attention(qk, (GQA_HEAD_DIM, 0),
                     [(qk, GQA_HEAD_DIM, lambda hh: GQA_HEADS + hh // group)],
                     uv, lambda hh: fw // GQA_HEAD_DIM + hh // group,
                     n_batch=n_batch, n_tok=n_tok, n_heads=GQA_HEADS, dv=GQA_HEAD_DIM, row_chunk=GQA_ROW_CHUNK,
                     name="gqa_attention")

    merged = _merge(o_a, o_b, o_c, wb_bf, layer, gates, d_model)
    otn = _tile(d_model, 512)
    if joined:
        nb_a = x[0].shape[0] // tm
        res_extras = [(x[0], (tm, otn), lambda i, j: (jnp.minimum(i, nb_a - 1), j)),
                      (x[1], (tm, otn), lambda i, j: (jnp.maximum(i - nb_a, 0), j))]
        res_epi = functools.partial(_epi_residual_join, nb_a=nb_a)
    else:
        res_extras = [(x, (tm, otn), lambda i, j: (i, j))]
        res_epi = _epi_residual
    (x,) = _mm(merged, wo_bf, res_epi, [(d_model, otn, F32)], extras=res_extras,
               tm=tm, tn=512, layer=layer, name="out_proj")

    h2 = _rmsnorm(x, g_mlp, BF16, "rmsnorm_mlp")
    d_ff = wup_bf.shape[2]
    (act,) = _mm(h2, wup_bf, _epi_relu2, [(d_ff, _tile(d_ff, 1024), BF16)],
                 tm=tm, tn=1024, layer=layer, name="mlp_up")
    (x,) = _mm(act, wdown_bf, _epi_residual, [(d_model, _tile(d_model, 1024), F32)],
               extras=[(x, (tm, _tile(d_model, 1024)), lambda i, j: (i, j))],
               tm=tm, tn=1024, tk=4096, layer=layer, name="mlp_down")
    return x


def kernel(x_prompt, x_sample, w_in, g_attn, g_qa, w_uq, g_kva, w_ukv, g_qk_q, g_qk_k,
           w_branch, w_o, g_mlp, w_up, w_down, g_final):
    n_tok, d_model = x_prompt.shape[1], x_prompt.shape[2]
    assert x_sample.shape[1:] == (n_tok, d_model)
    b_p, b_s = x_prompt.shape[0], x_sample.shape[0]
    n_batch = b_p + b_s
    x = (x_prompt.reshape(b_p * n_tok, d_model), x_sample.reshape(b_s * n_tok, d_model))

    mla_scale = (MLA_NOPE + MLA_ROPE) ** -0.5 * LOG2E
    heads_per_chunk = MXU_COLS // GQA_HEAD_DIM
    ones_bd = jnp.kron(jnp.eye(heads_per_chunk, dtype=F32), jnp.ones((GQA_HEAD_DIM, GQA_HEAD_DIM), F32))
    ones_bd = jnp.tile(ones_bd, (2, 1)).astype(BF16)
    tabs = (_rope_tables(n_tok, MLA_ROPE, 1.0), _rope_tables(n_tok, MLA_ROPE, mla_scale),
            _rope_tables(n_tok, GQA_HEAD_DIM, 1.0), ones_bd)
    fnet_tabs = _fnet_tables(n_tok, w_branch.shape[2] // FNET_GROUPS)
    wb_bf, wo_bf = w_branch.astype(BF16), w_o.astype(BF16)
    wup_bf, wdown_bf = w_up.astype(BF16), w_down.astype(BF16)
    for l in range(w_in.shape[0]):
        x = _layer(x, l, n_batch, n_tok, tabs, fnet_tabs, w_in[l], g_attn[l], g_qa[l], w_uq[l], g_kva[l],
                   w_ukv[l], g_qk_q[l], g_qk_k[l], wb_bf, wo_bf, g_mlp[l], wup_bf, wdown_bf)
    y_p, y_s = _rmsnorm_split(x, g_final, b_p * n_tok, F32, "rmsnorm_final")
    return (y_p.reshape(b_p, n_tok, d_model), y_s.reshape(b_s, n_tok, d_model))
```

```python
import functools
import math

import jax
import jax.numpy as jnp
from jax import lax
from jax.experimental import pallas as pl
from jax.experimental.pallas import tpu as pltpu

F32 = jnp.float32
BF16 = jnp.bfloat16

GRID_W = 64
ROPE_THETA = 10000.0
EPS = 1e-6
MLA_HEADS = 16
MLA_NOPE = 128
MLA_ROPE = 64
MLA_V = 128
FNET_GROUPS = 4
GQA_HEADS = 16
GQA_KV_HEADS = 4
GQA_HEAD_DIM = 128
N_BRANCH = 3

LANES = 128
MXU_COLS = 256
MLA_QK_PAD = 2 * LANES
VMEM_LIMIT_BYTES = 60 * 2 ** 20
LOG2E = math.log2(math.e)
ATTN_Q_BLOCK = 2048
MLA_ROW_CHUNK = 256
GQA_ROW_CHUNK = 128
MLA_HEADS_PER_STEP = 4
GQA_HEADS_PER_STEP = 4
MERGE_ROWS = 1024
MERGE_COLS = 512


def _tile(n, pref):
    if n <= pref:
        return n
    t = (pref // LANES) * LANES
    while t > LANES and n % t:
        t -= LANES
    assert n % t == 0, (n, pref)
    return t


def _params(*sem):
    return pltpu.CompilerParams(dimension_semantics=sem, vmem_limit_bytes=VMEM_LIMIT_BYTES)


def _rms(x, g):
    return (x * lax.rsqrt(jnp.mean(x * x, axis=-1, keepdims=True) + EPS)) * g


def _rmsnorm_kernel(x_ref, g_ref, o_ref):
    o_ref[...] = _rms(x_ref[...], g_ref[...]).astype(o_ref.dtype)


def _rmsnorm_join_kernel(xa_ref, xb_ref, g_ref, o_ref, *, nb_a):
    i = pl.program_id(0)

    @pl.when(i < nb_a)
    def _():
        o_ref[...] = _rms(xa_ref[...], g_ref[...]).astype(o_ref.dtype)

    @pl.when(i >= nb_a)
    def _():
        o_ref[...] = _rms(xb_ref[...], g_ref[...]).astype(o_ref.dtype)


def _rmsnorm_split_kernel(x_ref, g_ref, oa_ref, ob_ref, *, nb_a):
    i = pl.program_id(0)
    y = _rms(x_ref[...], g_ref[...]).astype(oa_ref.dtype)

    @pl.when(i < nb_a)
    def _():
        oa_ref[...] = y

    @pl.when(i >= nb_a)
    def _():
        ob_ref[...] = y


def _rmsnorm(x, g, out_dtype, name):
    t, d = x.shape
    tm = _tile(t, 256)
    return pl.pallas_call(
        _rmsnorm_kernel,
        grid=(t // tm,),
        in_specs=[pl.BlockSpec((tm, d), lambda i: (i, 0)),
                  pl.BlockSpec((1, d), lambda i: (0, 0))],
        out_specs=pl.BlockSpec((tm, d), lambda i: (i, 0)),
        out_shape=jax.ShapeDtypeStruct((t, d), out_dtype),
        compiler_params=_params("parallel"),
        name=name,
    )(x, g.reshape(1, d))


def _rmsnorm_join(xa, xb, g, out_dtype, name):
    (ta, d), tb = xa.shape, xb.shape[0]
    tm = _tile(math.gcd(ta, tb), 256)
    nb_a = ta // tm
    return pl.pallas_call(
        functools.partial(_rmsnorm_join_kernel, nb_a=nb_a),
        grid=((ta + tb) // tm,),
        in_specs=[pl.BlockSpec((tm, d), lambda i: (jnp.minimum(i, nb_a - 1), 0)),
                  pl.BlockSpec((tm, d), lambda i: (jnp.maximum(i - nb_a, 0), 0)),
                  pl.BlockSpec((1, d), lambda i: (0, 0))],
        out_specs=pl.BlockSpec((tm, d), lambda i: (i, 0)),
        out_shape=jax.ShapeDtypeStruct((ta + tb, d), out_dtype),
        compiler_params=_params("arbitrary"),
        name=name,
    )(xa, xb, g.reshape(1, d))


def _rmsnorm_split(x, g, ta, out_dtype, name):
    t, d = x.shape
    tm = _tile(math.gcd(ta, t - ta), 256)
    nb_a = ta // tm
    return pl.pallas_call(
        functools.partial(_rmsnorm_split_kernel, nb_a=nb_a),
        grid=(t // tm,),
        in_specs=[pl.BlockSpec((tm, d), lambda i: (i, 0)),
                  pl.BlockSpec((1, d), lambda i: (0, 0))],
        out_specs=[pl.BlockSpec((tm, d), lambda i: (jnp.minimum(i, nb_a - 1), 0)),
                   pl.BlockSpec((tm, d), lambda i: (jnp.maximum(i - nb_a, 0), 0))],
        out_shape=[jax.ShapeDtypeStruct((ta, d), out_dtype),
                   jax.ShapeDtypeStruct((t - ta, d), out_dtype)],
        compiler_params=_params("arbitrary"),
        name=name,
    )(x, g.reshape(1, d))


def _mm_kernel(a_ref, w_ref, *refs, epi, n_extra, n_out, nk, chunks):
    extras = refs[:n_extra]
    outs = refs[n_extra:n_extra + n_out]
    if nk == 1:
        for c0, cw in chunks:
            acc = jnp.dot(a_ref[...], w_ref[:, c0:c0 + cw], preferred_element_type=F32)
            epi(acc, c0, extras, outs)
        return
    assert epi is _epi_residual
    k = pl.program_id(2)
    d = jnp.dot(a_ref[...], w_ref[...], preferred_element_type=F32)

    @pl.when(k == 0)
    def _():
        outs[0][...] = extras[0][...] + d

    @pl.when(k > 0)
    def _():
        outs[0][...] += d


def _mm(a, w, epi, outs, extras=(), *, tm, tn, tk=None, chunk=None, chunks=None, layer=None,
        resident_w=False, name):
    m, kdim = a.shape
    n = w.shape[-1]
    tm = _tile(m, tm)
    tn = _tile(n, tn)
    tk = kdim if tk is None else _tile(kdim, tk)
    nk = kdim // tk
    if chunks is None:
        cw = tn if chunk is None else min(chunk, tn)
        chunks = [(c0, cw) for c0 in range(0, tn, cw)]
    assert nk == 1 or len(chunks) == 1
    grid = (m // tm, n // tn, nk)
    w_mode = dict(pipeline_mode=pl.Buffered(1)) if resident_w else {}
    assert not resident_w or (n == tn and nk == 1)
    if w.ndim == 3:
        w_spec = pl.BlockSpec((None, tk, tn), lambda i, j, k: (layer, k, j), **w_mode)
    else:
        w_spec = pl.BlockSpec((tk, tn), lambda i, j, k: (k, j), **w_mode)
    in_specs = [pl.BlockSpec((tm, tk), lambda i, j, k: (i, k)), w_spec]
    for _, bshape, imap in extras:
        in_specs.append(pl.BlockSpec(bshape, lambda i, j, k, imap=imap: imap(i, j)))
    out_specs = [pl.BlockSpec((tm, bc), lambda i, j, k: (i, j)) for _, bc, _ in outs]
    out_shape = [jax.ShapeDtypeStruct((m, tc), dt) for tc, _, dt in outs]
    return pl.pallas_call(
        functools.partial(_mm_kernel, epi=epi, n_extra=len(extras), n_out=len(outs), nk=nk,
                          chunks=chunks),
        grid=grid,
        in_specs=in_specs,
        out_specs=out_specs,
        out_shape=out_shape,
        compiler_params=_params("parallel", "parallel", "arbitrary"),
        name=name,
    )(a, w, *[e[0] for e in extras])


def _cols(c0, acc):
    return slice(c0, c0 + acc.shape[1])


def _epi_cast(acc, c0, extras, outs):
    outs[0][:, _cols(c0, acc)] = acc.astype(outs[0].dtype)


def _epi_relu2(acc, c0, extras, outs):
    r = jnp.maximum(acc, 0.0)
    outs[0][:, _cols(c0, acc)] = (r * r).astype(outs[0].dtype)


def _epi_residual(acc, c0, extras, outs):
    cs = _cols(c0, acc)
    outs[0][:, cs] = extras[0][:, cs] + acc


def _epi_residual_join(acc, c0, extras, outs, *, nb_a):
    cs = _cols(c0, acc)
    i = pl.program_id(0)

    @pl.when(i < nb_a)
    def _():
        outs[0][:, cs] = extras[0][:, cs] + acc

    @pl.when(i >= nb_a)
    def _():
        outs[0][:, cs] = extras[1][:, cs] + acc


def _rope(x, c, s):
    return x * c + pltpu.roll(x, LANES // 2, 1) * s


def _epi_latent(acc, c0, extras, outs, *, q_lora, kv_lora):
    gq, gkv, c, s = extras
    if c0 == 0:
        outs[0][...] = _rms(acc, gq[...]).astype(BF16)
    elif c0 == q_lora:
        outs[1][...] = _rms(acc, gkv[...]).astype(BF16)
    else:
        outs[2][...] = _rope(acc, c[...], s[...]).astype(BF16)


def _epi_gqa_heads(acc, c0, extras, outs):
    g, c, s, ones_bd = extras
    cv, sv = c[...], s[...]
    for g0 in range(0, acc.shape[1], MXU_COLS):
        x = acc[:, g0:g0 + MXU_COLS]
        x2 = x * x
        hi = x2.astype(BF16)
        lo = (x2 - hi.astype(F32)).astype(BF16)
        ss = jnp.dot(jnp.concatenate([hi, lo], axis=1), ones_bd[...], preferred_element_type=F32)
        y = (x * lax.rsqrt(ss * (1.0 / GQA_HEAD_DIM) + EPS)) * g[:, c0 + g0:c0 + g0 + MXU_COLS]
        for h0 in range(0, MXU_COLS, GQA_HEAD_DIM):
            osl = slice(c0 + g0 + h0, c0 + g0 + h0 + GQA_HEAD_DIM)
            outs[0][:, osl] = _rope(y[:, h0:h0 + GQA_HEAD_DIM], cv, sv).astype(BF16)


def _epi_mla_q(acc, c0, extras, outs, *, scale):
    c, s = extras
    cv, sv = c[...], s[...]
    for h in range(acc.shape[1] // MLA_QK_PAD):
        lo = slice(h * MLA_QK_PAD, h * MLA_QK_PAD + LANES)
        hi = slice(h * MLA_QK_PAD + LANES, (h + 1) * MLA_QK_PAD)
        olo = slice(c0 + lo.start, c0 + lo.stop)
        ohi = slice(c0 + hi.start, c0 + hi.stop)
        outs[0][:, olo] = (acc[:, lo] * scale).astype(BF16)
        outs[0][:, ohi] = _rope(acc[:, hi], cv, sv).astype(BF16)


def _attn_kernel(q_ref, *refs, n_k, k_per_head, v_per_head, hps, rows):
    k_refs, v_ref, o_ref = refs[:n_k], refs[n_k], refs[n_k + 1]
    qc = q_ref.shape[1] // hps
    dv = o_ref.shape[1] // hps

    def head_kv(hh):
        parts = [r[:, hh * LANES:(hh + 1) * LANES] if ph else r[...] for r, ph in zip(k_refs, k_per_head)]
        k = parts[0] if len(parts) == 1 else jnp.concatenate(parts, axis=-1)
        v = v_ref[:, hh * dv:(hh + 1) * dv] if v_per_head else v_ref[...]
        return k, jnp.concatenate([v, jnp.ones_like(v)], axis=-1)

    shared = not (any(k_per_head) or v_per_head)
    if shared:
        k, v1 = head_kv(0)
    for hh in range(hps):
        if not shared:
            k, v1 = head_kv(hh)
        for r0 in range(0, q_ref.shape[0], rows):
            q = q_ref[r0:r0 + rows, hh * qc:(hh + 1) * qc]
            s = lax.dot_general(q, k, (((1,), (1,)), ((), ())), preferred_element_type=F32)
            m = jnp.max(s, axis=-1, keepdims=True)
            p = jnp.exp2(s - m).astype(BF16)
            o = jnp.dot(p, v1, preferred_element_type=F32)
            o_ref[r0:r0 + rows, hh * dv:(hh + 1) * dv] = (o[:, :dv] / o[:, dv:]).astype(o_ref.dtype)


def _attention(q, q_cols, ks, v, v_spec, *, n_batch, n_tok, n_heads, dv, hps, row_chunk, name):
    tq = _tile(n_tok, ATTN_Q_BLOCK)
    nq = n_tok // tq
    rows = min(tq, row_chunk)
    assert n_heads % hps == 0
    in_specs = [pl.BlockSpec((tq, hps * q_cols), lambda b, h, i: (b * nq + i, h))]
    args = [q]
    for arr, per_head, colfn in ks:
        cols = hps * LANES if per_head else LANES
        in_specs.append(pl.BlockSpec((n_tok, cols), lambda b, h, i, colfn=colfn: (b, colfn(h))))
        args.append(arr)
    v_per_head, v_colfn = v_spec
    in_specs.append(pl.BlockSpec((n_tok, hps * dv if v_per_head else dv), lambda b, h, i: (b, v_colfn(h))))
    args.append(v)
    return pl.pallas_call(
        functools.partial(_attn_kernel, n_k=len(ks), k_per_head=tuple(k[1] for k in ks),
                          v_per_head=v_per_head, hps=hps, rows=rows),
        grid=(n_batch, n_heads // hps, nq),
        in_specs=in_specs,
        out_specs=pl.BlockSpec((tq, hps * dv), lambda b, h, i: (b * nq + i, h)),
        out_shape=jax.ShapeDtypeStruct((n_batch * n_tok, n_heads * dv), BF16),
        compiler_params=_params("parallel", "parallel", "parallel"),
        name=name,
    )(*args)


def _fnet_chan_kernel(u_ref, w_ref, yc_ref, ys_ref):
    gd = yc_ref.shape[1]
    y = jnp.dot(u_ref[...], w_ref[...], preferred_element_type=F32)
    yc_ref[...] = y[:, :gd].astype(BF16)
    ys_ref[...] = y[:, gd:].astype(BF16)


def _fnet_pos_kernel(cn_ref, sn_ref, yc_ref, ys_ref, o_ref):
    o = jnp.dot(cn_ref[...], yc_ref[...], preferred_element_type=F32)
    o = o + jnp.dot(sn_ref[...], ys_ref[...], preferred_element_type=F32)
    o_ref[...] = o.astype(o_ref.dtype)


def _dft_tables(n):
    idx = jnp.arange(n, dtype=jnp.int32)
    jk = (idx[:, None] * idx[None, :]) % n
    ang = jk.astype(F32) * (2.0 * math.pi / n)
    return jnp.cos(ang), jnp.sin(ang)


def _fnet_tables(n_tok, gd):
    norm = 1.0 / math.sqrt(n_tok * gd)
    s_chan = 2.0 ** round(math.log2(norm) / 2)
    s_pos = norm / s_chan
    cc, sc = _dft_tables(gd)
    w_chan = (jnp.concatenate([cc, sc], axis=1) * s_chan).astype(BF16)
    cn, sn = _dft_tables(n_tok)
    return w_chan, (cn * s_pos).astype(BF16), (sn * (-s_pos)).astype(BF16)


def _fnet(z, u_col0, tables, *, n_batch, n_tok, gd):
    t = z.shape[0]
    width = FNET_GROUPS * gd
    w_chan, cn, msn = tables
    tm = _tile(t, 2048)
    ublk = u_col0 // gd
    assert u_col0 % gd == 0
    yc, ys = pl.pallas_call(
        _fnet_chan_kernel,
        grid=(t // tm, FNET_GROUPS),
        in_specs=[pl.BlockSpec((tm, gd), lambda i, g: (i, ublk + g)),
                  pl.BlockSpec((gd, 2 * gd), lambda i, g: (0, 0))],
        out_specs=[pl.BlockSpec((tm, gd), lambda i, g: (i, g))] * 2,
        out_shape=[jax.ShapeDtypeStruct((t, width), BF16)] * 2,
        compiler_params=_params("parallel", "parallel"),
        name="fnet_chan",
    )(z, w_chan)

    tp = _tile(n_tok, 1024)
    tn = _tile(width, 1024)
    npb = n_tok // tp
    return pl.pallas_call(
        _fnet_pos_kernel,
        grid=(n_batch, width // tn, npb),
        in_specs=[pl.BlockSpec((tp, n_tok), lambda b, j, i: (i, 0)),
                  pl.BlockSpec((tp, n_tok), lambda b, j, i: (i, 0)),
                  pl.BlockSpec((n_tok, tn), lambda b, j, i: (b, j)),
                  pl.BlockSpec((n_tok, tn), lambda b, j, i: (b, j))],
        out_specs=pl.BlockSpec((tp, tn), lambda b, j, i: (b * npb + i, j)),
        out_shape=jax.ShapeDtypeStruct((t, width), BF16),
        compiler_params=_params("parallel", "parallel", "parallel"),
        name="fnet_pos",
    )(cn, msn, yc, ys)


def _merge_kernel(oa_ref, ob_ref, oc_ref, w_ref, ga_ref, gb_ref, gc_ref, out_ref):
    tn = out_ref.shape[1]
    cw = min(tn, MXU_COLS)
    for c0 in range(0, tn, cw):
        cs = slice(c0, c0 + cw)
        acc = None
        for b, (o_ref, g_ref) in enumerate(((oa_ref, ga_ref), (ob_ref, gb_ref), (oc_ref, gc_ref))):
            gate = 1.0 / (1.0 + jnp.exp(-g_ref[:, cs].astype(F32)))
            c = gate * jnp.dot(o_ref[...], w_ref[b, :, cs], preferred_element_type=F32)
            acc = c if acc is None else acc + c
        out_ref[:, cs] = acc.astype(out_ref.dtype)


def _merge(oa, ob, oc, w_branch, layer, gates, d_model):
    t, bw = oa.shape
    tm = _tile(t, MERGE_ROWS)
    tn = _tile(d_model, MERGE_COLS)
    nj = d_model // tn
    o_spec = pl.BlockSpec((tm, bw), lambda i, j: (i, 0))
    g_specs = [pl.BlockSpec((tm, tn), lambda i, j, b=b: (i, b * nj + j)) for b in range(N_BRANCH)]
    return pl.pallas_call(
        _merge_kernel,
        grid=(t // tm, nj),
        in_specs=[o_spec, o_spec, o_spec,
                  pl.BlockSpec((None, N_BRANCH, bw, tn), lambda i, j: (layer, 0, 0, j))] + g_specs,
        out_specs=pl.BlockSpec((tm, tn), lambda i, j: (i, j)),
        out_shape=jax.ShapeDtypeStruct((t, d_model), BF16),
        compiler_params=_params("parallel", "parallel"),
        name="gated_merge",
    )(oa, ob, oc, w_branch, gates, gates, gates)


def _rope_tables(n_tok, rot_dim, scale):
    rows = n_tok // GRID_W
    row_idx = jnp.broadcast_to(jnp.arange(rows)[:, None], (rows, GRID_W)).reshape(-1).astype(F32)
    col_idx = jnp.broadcast_to(jnp.arange(GRID_W)[None, :], (rows, GRID_W)).reshape(-1).astype(F32)
    nq = rot_dim // 4
    freqs = ROPE_THETA ** (-(2.0 * jnp.arange(nq, dtype=F32)) / (rot_dim // 2))
    ang = jnp.stack([row_idx[:, None] * freqs, col_idx[:, None] * freqs], axis=1)
    cos, sin = jnp.cos(ang), jnp.sin(ang)
    pad = jnp.zeros((n_tok, LANES // 2 - 2 * nq), F32)
    cblk = jnp.concatenate([cos[:, 0], cos[:, 1], pad], axis=1)
    sblk = jnp.concatenate([sin[:, 0], sin[:, 1], pad], axis=1)
    c = jnp.concatenate([cblk, cblk], axis=1)
    s = jnp.concatenate([-sblk, sblk], axis=1)
    return c * scale, s * scale


def _rotary_lanes(w, rot_dim):
    nq = rot_dim // 4
    lead = w.shape[:-1]
    r = jnp.swapaxes(w.reshape(*lead, 2, 2, nq), -3, -2).reshape(*lead, 2, 2 * nq)
    r = jnp.pad(r, [(0, 0)] * (len(lead) + 1) + [(0, LANES // 2 - 2 * nq)])
    return r.reshape(*lead, LANES)


def _layer_weights(w_in, w_uq, w_ukv, g_qk_q, g_qk_k, q_lora, kv_lora, d_model, fw, gqa_scale):
    seg = [q_lora, kv_lora, MLA_ROPE, fw, GQA_HEADS * GQA_HEAD_DIM,
           GQA_KV_HEADS * GQA_HEAD_DIM, GQA_KV_HEADS * GQA_HEAD_DIM, N_BRANCH * d_model]
    offs = [0]
    for s in seg:
        offs.append(offs[-1] + s)
    cq, ckv, kpe, uf, qc, kc, vc, gl = [w_in[:, offs[i]:offs[i + 1]] for i in range(8)]
    w_gate = gl.astype(BF16)
    w_uv = jnp.concatenate([uf, vc], axis=1).astype(BF16)
    d_in = w_in.shape[0]
    qc = _rotary_lanes(qc.reshape(d_in, GQA_HEADS, GQA_HEAD_DIM), GQA_HEAD_DIM).reshape(d_in, -1)
    kc = _rotary_lanes(kc.reshape(d_in, GQA_KV_HEADS, GQA_HEAD_DIM), GQA_HEAD_DIM).reshape(d_in, -1)
    w_qk = jnp.concatenate([qc, kc], axis=1).astype(BF16)
    w_lat = jnp.concatenate([cq, ckv, _rotary_lanes(kpe, MLA_ROPE)], axis=1).astype(BF16)
    g_qk = jnp.concatenate([jnp.tile(_rotary_lanes(g_qk_q, GQA_HEAD_DIM) * gqa_scale, GQA_HEADS),
                            jnp.tile(_rotary_lanes(g_qk_k, GQA_HEAD_DIM), GQA_KV_HEADS)]).reshape(1, -1)
    uq = w_uq.reshape(q_lora, MLA_HEADS, MLA_NOPE + MLA_ROPE)
    uq = jnp.concatenate([uq[:, :, :MLA_NOPE], _rotary_lanes(uq[:, :, MLA_NOPE:], MLA_ROPE)], axis=2)
    w_uq_p = uq.reshape(q_lora, MLA_HEADS * MLA_QK_PAD).astype(BF16)
    ukv = w_ukv.reshape(kv_lora, MLA_HEADS, MLA_NOPE + MLA_V)
    w_ukv_p = jnp.concatenate([ukv[:, :, :MLA_NOPE].reshape(kv_lora, -1),
                               ukv[:, :, MLA_NOPE:].reshape(kv_lora, -1)], axis=1).astype(BF16)
    return w_gate, w_uv, w_qk, w_lat, g_qk, w_uq_p, w_ukv_p


def _layer(x, layer, n_batch, n_tok, tabs, fnet_tabs, w_in, g_attn, g_qa, w_uq, g_kva, w_ukv,
           g_qk_q, g_qk_k, wb_bf, wo_bf, g_mlp, wup_bf, wdown_bf):
    joined = isinstance(x, tuple)
    t = sum(a.shape[0] for a in x) if joined else x.shape[0]
    d_model = g_attn.shape[0]
    q_lora, kv_lora = g_qa.shape[0], g_kva.shape[0]
    fw = wb_bf.shape[2]
    gd = fw // FNET_GROUPS
    mla_scale = (MLA_NOPE + MLA_ROPE) ** -0.5 * LOG2E
    gqa_scale = GQA_HEAD_DIM ** -0.5 * LOG2E
    (ca, sa), (ca_s, sa_s), (cc, sc), ones_bd = tabs
    w_gate, w_uv, w_qk, w_lat, g_qk, w_uq_p, w_ukv_p = _layer_weights(
        w_in, w_uq, w_ukv, g_qk_q, g_qk_k, q_lora, kv_lora, d_model, fw, gqa_scale)

    tm = _tile(n_tok, 1024)
    nrb = n_tok // tm
    tab_map = lambda i, j: (i % nrb, 0)

    if joined:
        h = _rmsnorm_join(x[0], x[1], g_attn, BF16, "rmsnorm_attn")
    else:
        h = _rmsnorm(x, g_attn, BF16, "rmsnorm_attn")

    (gates,) = _mm(h, w_gate, _epi_cast, [(w_gate.shape[1], _tile(w_gate.shape[1], 1024), BF16)],
                   tm=tm, tn=1024, name="in_proj_gates")
    (uv,) = _mm(h, w_uv, _epi_cast, [(w_uv.shape[1], _tile(w_uv.shape[1], 1280), BF16)],
                tm=tm, tn=1280, name="in_proj_uv")
    qk_tn = _tile(w_qk.shape[1], 512)
    (qk,) = _mm(h, w_qk, _epi_gqa_heads, [(w_qk.shape[1], qk_tn, BF16)],
                extras=[(g_qk, (1, qk_tn), lambda i, j: (0, j)),
                        (cc, (tm, LANES), tab_map), (sc, (tm, LANES), tab_map),
                        (ones_bd, ones_bd.shape, lambda i, j: (0, 0))],
                tm=tm, tn=512, name="in_proj_gqa_qk")
    cqn, ckvn, kpe = _mm(
        h, w_lat, functools.partial(_epi_latent, q_lora=q_lora, kv_lora=kv_lora),
        [(q_lora, q_lora, BF16), (kv_lora, kv_lora, BF16), (LANES, LANES, BF16)],
        extras=[(g_qa.reshape(1, -1), (1, q_lora), lambda i, j: (0, 0)),
                (g_kva.reshape(1, -1), (1, kv_lora), lambda i, j: (0, 0)),
                (ca, (tm, LANES), tab_map), (sa, (tm, LANES), tab_map)],
        tm=tm, tn=w_lat.shape[1], resident_w=True,
        chunks=[(0, q_lora), (q_lora, kv_lora), (q_lora + kv_lora, LANES)], name="in_proj_latent")

    (q_a,) = _mm(cqn, w_uq_p, functools.partial(_epi_mla_q, scale=mla_scale),
                 [(w_uq_p.shape[1], w_uq_p.shape[1], BF16)],
                 extras=[(ca_s, (tm, LANES), tab_map), (sa_s, (tm, LANES), tab_map)],
                 tm=tm, tn=w_uq_p.shape[1], chunk=2 * MLA_QK_PAD, resident_w=True, name="mla_q_up")
    (kv_a,) = _mm(ckvn, w_ukv_p, _epi_cast, [(w_ukv_p.shape[1], w_ukv_p.shape[1], BF16)],
                  tm=tm, tn=w_ukv_p.shape[1], chunk=1024, resident_w=True, name="mla_kv_up")
    o_a = _attention(q_a, MLA_QK_PAD,
                     [(kv_a, True, lambda hs: hs), (kpe, False, lambda hs: 0)],
                     kv_a, (True, lambda hs: MLA_HEADS // MLA_HEADS_PER_STEP + hs),
                     n_batch=n_batch, n_tok=n_tok, n_heads=MLA_HEADS, dv=MLA_V, hps=MLA_HEADS_PER_STEP,
                     row_chunk=MLA_ROW_CHUNK, name="mla_attention")

    o_b = _fnet(uv, 0, fnet_tabs, n_batch=n_batch, n_tok=n_tok, gd=gd)

    group = GQA_HEADS // GQA_KV_HEADS
    hps = GQA_HEADS_PER_STEP
    assert group % hps == 0
    o_c = _attention(qk, GQA_HEAD_DIM,
                     [(qk, False, lambda hs: GQA_HEADS + (hs * hps) // group)],
                     uv, (False, lambda hs: fw // GQA_HEAD_DIM + (hs * hps) // group),
                     n_batch=n_batch, n_tok=n_tok, n_heads=GQA_HEADS, dv=GQA_HEAD_DIM, hps=hps,
                     row_chunk=GQA_ROW_CHUNK, name="gqa_attention")

    merged = _merge(o_a, o_b, o_c, wb_bf, layer, gates, d_model)
    otn = _tile(d_model, 512)
    if joined:
        nb_a = x[0].shape[0] // tm
        res_extras = [(x[0], (tm, otn), lambda i, j: (jnp.minimum(i, nb_a - 1), j)),
                      (x[1], (tm, otn), lambda i, j: (jnp.maximum(i - nb_a, 0), j))]
        res_epi = functools.partial(_epi_residual_join, nb_a=nb_a)
    else:
        res_extras = [(x, (tm, otn), lambda i, j: (i, j))]
        res_epi = _epi_residual
    (x,) = _mm(merged, wo_bf, res_epi, [(d_model, otn, F32)], extras=res_extras,
               tm=tm, tn=512, layer=layer, name="out_proj")

    h2 = _rmsnorm(x, g_mlp, BF16, "rmsnorm_mlp")
    d_ff = wup_bf.shape[2]
    (act,) = _mm(h2, wup_bf, _epi_relu2, [(d_ff, _tile(d_ff, 1024), BF16)],
                 tm=tm, tn=1024, layer=layer, name="mlp_up")
    (x,) = _mm(act, wdown_bf, _epi_residual, [(d_model, _tile(d_model, 1024), F32)],
               extras=[(x, (tm, _tile(d_model, 1024)), lambda i, j: (i, j))],
               tm=tm, tn=1024, tk=4096, layer=layer, name="mlp_down")
    return x


def kernel(x_prompt, x_sample, w_in, g_attn, g_qa, w_uq, g_kva, w_ukv, g_qk_q, g_qk_k,
           w_branch, w_o, g_mlp, w_up, w_down, g_final):
    n_tok, d_model = x_prompt.shape[1], x_prompt.shape[2]
    assert x_sample.shape[1:] == (n_tok, d_model)
    b_p, b_s = x_prompt.shape[0], x_sample.shape[0]
    n_batch = b_p + b_s
    x = (x_prompt.reshape(b_p * n_tok, d_model), x_sample.reshape(b_s * n_tok, d_model))

    mla_scale = (MLA_NOPE + MLA_ROPE) ** -0.5 * LOG2E
    heads_per_chunk = MXU_COLS // GQA_HEAD_DIM
    ones_bd = jnp.kron(jnp.eye(heads_per_chunk, dtype=F32), jnp.ones((GQA_HEAD_DIM, GQA_HEAD_DIM), F32))
    ones_bd = jnp.tile(ones_bd, (2, 1)).astype(BF16)
    tabs = (_rope_tables(n_tok, MLA_ROPE, 1.0), _rope_tables(n_tok, MLA_ROPE, mla_scale),
            _rope_tables(n_tok, GQA_HEAD_DIM, 1.0), ones_bd)
    fnet_tabs = _fnet_tables(n_tok, w_branch.shape[2] // FNET_GROUPS)
    wb_bf, wo_bf = w_branch.astype(BF16), w_o.astype(BF16)
    wup_bf, wdown_bf = w_up.astype(BF16), w_down.astype(BF16)
    for l in range(w_in.shape[0]):
        x = _layer(x, l, n_batch, n_tok, tabs, fnet_tabs, w_in[l], g_attn[l], g_qa[l], w_uq[l], g_kva[l],
                   w_ukv[l], g_qk_q[l], g_qk_k[l], wb_bf, wo_bf, g_mlp[l], wup_bf, wdown_bf)
    y_p, y_s = _rmsnorm_split(x, g_final, b_p * n_tok, F32, "rmsnorm_final")
    return (y_p.reshape(b_p, n_tok, d_model), y_s.reshape(b_s, n_tok, d_model))
```

```python
import functools
import math

import jax
import jax.numpy as jnp
from jax import lax
from jax.experimental import pallas as pl
from jax.experimental.pallas import tpu as pltpu

F32 = jnp.float32
BF16 = jnp.bfloat16

GRID_W = 64
ROPE_THETA = 10000.0
EPS = 1e-6
MLA_HEADS = 16
MLA_NOPE = 128
MLA_ROPE = 64
MLA_V = 128
FNET_GROUPS = 4
GQA_HEADS = 16
GQA_KV_HEADS = 4
GQA_HEAD_DIM = 128
N_BRANCH = 3

LANES = 128
BF16_SUBLANES = 16
MXU_COLS = 256
MLA_QK_PAD = 2 * LANES
VMEM_LIMIT_BYTES = 60 * 2 ** 20
LOG2E = math.log2(math.e)
ATTN_Q_BLOCK = 2048
MLA_ROW_CHUNK = 256
GQA_ROW_CHUNK = 128
MLA_HEADS_PER_STEP = 4
GQA_HEADS_PER_STEP = 4
MERGE_ROWS = 1024
MERGE_COLS = 512


def _tile(n, pref):
    if n <= pref:
        return n
    t = (pref // LANES) * LANES
    while t > LANES and n % t:
        t -= LANES
    assert n % t == 0, (n, pref)
    return t


def _cast_rows(rows, steps):
    br = BF16_SUBLANES
    while rows % br or rows // br > steps:
        br += BF16_SUBLANES
    return br


def _params(*sem):
    return pltpu.CompilerParams(dimension_semantics=sem, vmem_limit_bytes=VMEM_LIMIT_BYTES)


def _rms(x, g):
    return (x * lax.rsqrt(jnp.mean(x * x, axis=-1, keepdims=True) + EPS)) * g


def _rmsnorm_kernel(x_ref, g_ref, o_ref):
    o_ref[...] = _rms(x_ref[...], g_ref[...]).astype(o_ref.dtype)


def _rmsnorm_join_kernel(xa_ref, xb_ref, g_ref, o_ref, *, nb_a):
    i = pl.program_id(0)

    @pl.when(i < nb_a)
    def _():
        o_ref[...] = _rms(xa_ref[...], g_ref[...]).astype(o_ref.dtype)

    @pl.when(i >= nb_a)
    def _():
        o_ref[...] = _rms(xb_ref[...], g_ref[...]).astype(o_ref.dtype)


def _rmsnorm_split_kernel(x_ref, g_ref, oa_ref, ob_ref, *, nb_a):
    i = pl.program_id(0)
    y = _rms(x_ref[...], g_ref[...]).astype(oa_ref.dtype)

    @pl.when(i < nb_a)
    def _():
        oa_ref[...] = y

    @pl.when(i >= nb_a)
    def _():
        ob_ref[...] = y


def _rmsnorm(x, g, out_dtype, name):
    t, d = x.shape
    tm = _tile(t, 256)
    return pl.pallas_call(
        _rmsnorm_kernel,
        grid=(t // tm,),
        in_specs=[pl.BlockSpec((tm, d), lambda i: (i, 0)),
                  pl.BlockSpec((1, d), lambda i: (0, 0))],
        out_specs=pl.BlockSpec((tm, d), lambda i: (i, 0)),
        out_shape=jax.ShapeDtypeStruct((t, d), out_dtype),
        compiler_params=_params("parallel"),
        name=name,
    )(x, g.reshape(1, d))


def _rmsnorm_join(xa, xb, g, out_dtype, name):
    (ta, d), tb = xa.shape, xb.shape[0]
    tm = _tile(math.gcd(ta, tb), 256)
    nb_a = ta // tm
    return pl.pallas_call(
        functools.partial(_rmsnorm_join_kernel, nb_a=nb_a),
        grid=((ta + tb) // tm,),
        in_specs=[pl.BlockSpec((tm, d), lambda i: (jnp.minimum(i, nb_a - 1), 0)),
                  pl.BlockSpec((tm, d), lambda i: (jnp.maximum(i - nb_a, 0), 0)),
                  pl.BlockSpec((1, d), lambda i: (0, 0))],
        out_specs=pl.BlockSpec((tm, d), lambda i: (i, 0)),
        out_shape=jax.ShapeDtypeStruct((ta + tb, d), out_dtype),
        compiler_params=_params("arbitrary"),
        name=name,
    )(xa, xb, g.reshape(1, d))


def _rmsnorm_split(x, g, ta, out_dtype, name):
    t, d = x.shape
    tm = _tile(math.gcd(ta, t - ta), 256)
    nb_a = ta // tm
    return pl.pallas_call(
        functools.partial(_rmsnorm_split_kernel, nb_a=nb_a),
        grid=(t // tm,),
        in_specs=[pl.BlockSpec((tm, d), lambda i: (i, 0)),
                  pl.BlockSpec((1, d), lambda i: (0, 0))],
        out_specs=[pl.BlockSpec((tm, d), lambda i: (jnp.minimum(i, nb_a - 1), 0)),
                   pl.BlockSpec((tm, d), lambda i: (jnp.maximum(i - nb_a, 0), 0))],
        out_shape=[jax.ShapeDtypeStruct((ta, d), out_dtype),
                   jax.ShapeDtypeStruct((t - ta, d), out_dtype)],
        compiler_params=_params("arbitrary"),
        name=name,
    )(x, g.reshape(1, d))


def _mm_kernel(a_ref, w_ref, *refs, epi, n_extra, n_cast, n_out, nk, chunks):
    extras = refs[:n_extra]
    cast_in = refs[n_extra:n_extra + n_cast]
    outs = refs[n_extra + n_cast:n_extra + n_cast + n_out]
    cast_out = refs[n_extra + n_cast + n_out:]
    for ci, co in zip(cast_in, cast_out):
        co[...] = ci[...].astype(co.dtype)
    if nk == 1:
        for c0, cw in chunks:
            acc = jnp.dot(a_ref[...], w_ref[:, c0:c0 + cw], preferred_element_type=F32)
            epi(acc, c0, extras, outs)
        return
    assert epi is _epi_residual
    k = pl.program_id(2)
    d = jnp.dot(a_ref[...], w_ref[...], preferred_element_type=F32)

    @pl.when(k == 0)
    def _():
        outs[0][...] = extras[0][...] + d

    @pl.when(k > 0)
    def _():
        outs[0][...] += d


def _mm(a, w, epi, outs, extras=(), *, tm, tn, tk=None, chunk=None, chunks=None, layer=None,
        resident_w=False, casts=(), name):
    m, kdim = a.shape
    n = w.shape[-1]
    tm = _tile(m, tm)
    tn = _tile(n, tn)
    tk = kdim if tk is None else _tile(kdim, tk)
    nk = kdim // tk
    if chunks is None:
        cw = tn if chunk is None else min(chunk, tn)
        chunks = [(c0, cw) for c0 in range(0, tn, cw)]
    assert nk == 1 or len(chunks) == 1
    grid = (m // tm, n // tn, nk)
    w_mode = dict(pipeline_mode=pl.Buffered(1)) if resident_w else {}
    assert not resident_w or (n == tn and nk == 1)
    if w.ndim == 3:
        w_spec = pl.BlockSpec((None, tk, tn), lambda i, j, k: (layer, k, j), **w_mode)
    else:
        w_spec = pl.BlockSpec((tk, tn), lambda i, j, k: (k, j), **w_mode)
    in_specs = [pl.BlockSpec((tm, tk), lambda i, j, k: (i, k)), w_spec]
    for _, bshape, imap in extras:
        in_specs.append(pl.BlockSpec(bshape, lambda i, j, k, imap=imap: imap(i, j)))
    out_specs = [pl.BlockSpec((tm, bc), lambda i, j, k: (i, j)) for _, bc, _ in outs]
    out_shape = [jax.ShapeDtypeStruct((m, tc), dt) for tc, _, dt in outs]
    nj = n // tn
    for arr, lyr in casts:
        rows, cols = arr.shape[-2:]
        br = _cast_rows(rows, grid[0] * nj * nk)
        nb = rows // br
        blk = lambda i, j, k, nb=nb: jnp.minimum((i * nj + j) * nk + k, nb - 1)
        in_specs.append(pl.BlockSpec((None, br, cols), lambda i, j, k, lyr=lyr, blk=blk: (lyr, blk(i, j, k), 0)))
        out_specs.append(pl.BlockSpec((br, cols), lambda i, j, k, blk=blk: (blk(i, j, k), 0)))
        out_shape.append(jax.ShapeDtypeStruct((rows, cols), BF16))
    sem = ("arbitrary",) * 3 if casts else ("parallel", "parallel", "arbitrary")
    return pl.pallas_call(
        functools.partial(_mm_kernel, epi=epi, n_extra=len(extras), n_cast=len(casts), n_out=len(outs),
                          nk=nk, chunks=chunks),
        grid=grid,
        in_specs=in_specs,
        out_specs=out_specs,
        out_shape=out_shape,
        compiler_params=_params(*sem),
        name=name,
    )(a, w, *[e[0] for e in extras], *[c[0] for c in casts])


def _cols(c0, acc):
    return slice(c0, c0 + acc.shape[1])


def _epi_cast(acc, c0, extras, outs):
    outs[0][:, _cols(c0, acc)] = acc.astype(outs[0].dtype)


def _epi_relu2(acc, c0, extras, outs):
    r = jnp.maximum(acc, 0.0)
    outs[0][:, _cols(c0, acc)] = (r * r).astype(outs[0].dtype)


def _epi_residual(acc, c0, extras, outs):
    cs = _cols(c0, acc)
    outs[0][:, cs] = extras[0][:, cs] + acc


def _epi_residual_join(acc, c0, extras, outs, *, nb_a):
    cs = _cols(c0, acc)
    i = pl.program_id(0)

    @pl.when(i < nb_a)
    def _():
        outs[0][:, cs] = extras[0][:, cs] + acc

    @pl.when(i >= nb_a)
    def _():
        outs[0][:, cs] = extras[1][:, cs] + acc


def _rope(x, c, s):
    return x * c + pltpu.roll(x, LANES // 2, 1) * s


def _epi_latent(acc, c0, extras, outs, *, q_lora, kv_lora):
    gq, gkv, c, s = extras
    if c0 == 0:
        outs[0][...] = _rms(acc, gq[...]).astype(BF16)
    elif c0 == q_lora:
        outs[1][...] = _rms(acc, gkv[...]).astype(BF16)
    else:
        outs[2][...] = _rope(acc, c[...], s[...]).astype(BF16)


def _epi_gqa_heads(acc, c0, extras, outs):
    g, c, s, ones_bd = extras
    cv, sv = c[...], s[...]
    for g0 in range(0, acc.shape[1], MXU_COLS):
        x = acc[:, g0:g0 + MXU_COLS]
        x2 = x * x
        hi = x2.astype(BF16)
        lo = (x2 - hi.astype(F32)).astype(BF16)
        ss = jnp.dot(jnp.concatenate([hi, lo], axis=1), ones_bd[...], preferred_element_type=F32)
        y = (x * lax.rsqrt(ss * (1.0 / GQA_HEAD_DIM) + EPS)) * g[:, c0 + g0:c0 + g0 + MXU_COLS]
        for h0 in range(0, MXU_COLS, GQA_HEAD_DIM):
            osl = slice(c0 + g0 + h0, c0 + g0 + h0 + GQA_HEAD_DIM)
            outs[0][:, osl] = _rope(y[:, h0:h0 + GQA_HEAD_DIM], cv, sv).astype(BF16)


def _epi_mla_q(acc, c0, extras, outs, *, scale):
    c, s = extras
    cv, sv = c[...], s[...]
    for h in range(acc.shape[1] // MLA_QK_PAD):
        lo = slice(h * MLA_QK_PAD, h * MLA_QK_PAD + LANES)
        hi = slice(h * MLA_QK_PAD + LANES, (h + 1) * MLA_QK_PAD)
        olo = slice(c0 + lo.start, c0 + lo.stop)
        ohi = slice(c0 + hi.start, c0 + hi.stop)
        outs[0][:, olo] = (acc[:, lo] * scale).astype(BF16)
        outs[0][:, ohi] = _rope(acc[:, hi], cv, sv).astype(BF16)


def _attn_kernel(q_ref, *refs, n_k, k_per_head, v_per_head, hps, rows):
    k_refs, v_ref, o_ref = refs[:n_k], refs[n_k], refs[n_k + 1]
    qc = q_ref.shape[1] // hps
    dv = o_ref.shape[1] // hps

    def head_kv(hh):
        parts = [r[:, hh * LANES:(hh + 1) * LANES] if ph else r[...] for r, ph in zip(k_refs, k_per_head)]
        k = parts[0] if len(parts) == 1 else jnp.concatenate(parts, axis=-1)
        v = v_ref[:, hh * dv:(hh + 1) * dv] if v_per_head else v_ref[...]
        return k, jnp.concatenate([v, jnp.ones_like(v)], axis=-1)

    shared = not (any(k_per_head) or v_per_head)
    if shared:
        k, v1 = head_kv(0)
    for hh in range(hps):
        if not shared:
            k, v1 = head_kv(hh)
        for r0 in range(0, q_ref.shape[0], rows):
            q = q_ref[r0:r0 + rows, hh * qc:(hh + 1) * qc]
            s = lax.dot_general(q, k, (((1,), (1,)), ((), ())), preferred_element_type=F32)
            m = jnp.max(s, axis=-1, keepdims=True)
            p = jnp.exp2(s - m).astype(BF16)
            o = jnp.dot(p, v1, preferred_element_type=F32)
            o_ref[r0:r0 + rows, hh * dv:(hh + 1) * dv] = (o[:, :dv] / o[:, dv:]).astype(o_ref.dtype)


def _attention(q, q_cols, ks, v, v_spec, *, n_batch, n_tok, n_heads, dv, hps, row_chunk, name):
    tq = _tile(n_tok, ATTN_Q_BLOCK)
    nq = n_tok // tq
    rows = min(tq, row_chunk)
    assert n_heads % hps == 0
    in_specs = [pl.BlockSpec((tq, hps * q_cols), lambda b, h, i: (b * nq + i, h))]
    args = [q]
    for arr, per_head, colfn in ks:
        cols = hps * LANES if per_head else LANES
        in_specs.append(pl.BlockSpec((n_tok, cols), lambda b, h, i, colfn=colfn: (b, colfn(h))))
        args.append(arr)
    v_per_head, v_colfn = v_spec
    in_specs.append(pl.BlockSpec((n_tok, hps * dv if v_per_head else dv), lambda b, h, i: (b, v_colfn(h))))
    args.append(v)
    return pl.pallas_call(
        functools.partial(_attn_kernel, n_k=len(ks), k_per_head=tuple(k[1] for k in ks),
                          v_per_head=v_per_head, hps=hps, rows=rows),
        grid=(n_batch, n_heads // hps, nq),
        in_specs=in_specs,
        out_specs=pl.BlockSpec((tq, hps * dv), lambda b, h, i: (b * nq + i, h)),
        out_shape=jax.ShapeDtypeStruct((n_batch * n_tok, n_heads * dv), BF16),
        compiler_params=_params("parallel", "parallel", "parallel"),
        name=name,
    )(*args)


def _fnet_chan_kernel(u_ref, w_ref, yc_ref, ys_ref):
    gd = yc_ref.shape[1]
    y = jnp.dot(u_ref[...], w_ref[...], preferred_element_type=F32)
    yc_ref[...] = y[:, :gd].astype(BF16)
    ys_ref[...] = y[:, gd:].astype(BF16)


def _fnet_pos_kernel(cn_ref, sn_ref, yc_ref, ys_ref, o_ref):
    o = jnp.dot(cn_ref[...], yc_ref[...], preferred_element_type=F32)
    o = o + jnp.dot(sn_ref[...], ys_ref[...], preferred_element_type=F32)
    o_ref[...] = o.astype(o_ref.dtype)


def _dft_tables(n):
    idx = jnp.arange(n, dtype=jnp.int32)
    jk = (idx[:, None] * idx[None, :]) % n
    ang = jk.astype(F32) * (2.0 * math.pi / n)
    return jnp.cos(ang), jnp.sin(ang)


def _fnet_tables(n_tok, gd):
    norm = 1.0 / math.sqrt(n_tok * gd)
    s_chan = 2.0 ** round(math.log2(norm) / 2)
    s_pos = norm / s_chan
    cc, sc = _dft_tables(gd)
    w_chan = (jnp.concatenate([cc, sc], axis=1) * s_chan).astype(BF16)
    cn, sn = _dft_tables(n_tok)
    return w_chan, (cn * s_pos).astype(BF16), (sn * (-s_pos)).astype(BF16)


def _fnet(z, u_col0, tables, *, n_batch, n_tok, gd):
    t = z.shape[0]
    width = FNET_GROUPS * gd
    w_chan, cn, msn = tables
    tm = _tile(t, 2048)
    ublk = u_col0 // gd
    assert u_col0 % gd == 0
    yc, ys = pl.pallas_call(
        _fnet_chan_kernel,
        grid=(t // tm, FNET_GROUPS),
        in_specs=[pl.BlockSpec((tm, gd), lambda i, g: (i, ublk + g)),
                  pl.BlockSpec((gd, 2 * gd), lambda i, g: (0, 0))],
        out_specs=[pl.BlockSpec((tm, gd), lambda i, g: (i, g))] * 2,
        out_shape=[jax.ShapeDtypeStruct((t, width), BF16)] * 2,
        compiler_params=_params("parallel", "parallel"),
        name="fnet_chan",
    )(z, w_chan)

    tp = _tile(n_tok, 1024)
    tn = _tile(width, 1024)
    npb = n_tok // tp
    return pl.pallas_call(
        _fnet_pos_kernel,
        grid=(n_batch, width // tn, npb),
        in_specs=[pl.BlockSpec((tp, n_tok), lambda b, j, i: (i, 0)),
                  pl.BlockSpec((tp, n_tok), lambda b, j, i: (i, 0)),
                  pl.BlockSpec((n_tok, tn), lambda b, j, i: (b, j)),
                  pl.BlockSpec((n_tok, tn), lambda b, j, i: (b, j))],
        out_specs=pl.BlockSpec((tp, tn), lambda b, j, i: (b * npb + i, j)),
        out_shape=jax.ShapeDtypeStruct((t, width), BF16),
        compiler_params=_params("parallel", "parallel", "parallel"),
        name="fnet_pos",
    )(cn, msn, yc, ys)


def _merge_kernel(oa_ref, ob_ref, oc_ref, w_ref, ga_ref, gb_ref, gc_ref, out_ref):
    tn = out_ref.shape[1]
    cw = min(tn, MXU_COLS)
    for c0 in range(0, tn, cw):
        cs = slice(c0, c0 + cw)
        acc = None
        for b, (o_ref, g_ref) in enumerate(((oa_ref, ga_ref), (ob_ref, gb_ref), (oc_ref, gc_ref))):
            gate = 1.0 / (1.0 + jnp.exp(-g_ref[:, cs].astype(F32)))
            c = gate * jnp.dot(o_ref[...], w_ref[b, :, cs], preferred_element_type=F32)
            acc = c if acc is None else acc + c
        out_ref[:, cs] = acc.astype(out_ref.dtype)


def _merge(oa, ob, oc, w_branch, gates, d_model):
    t, bw = oa.shape
    tm = _tile(t, MERGE_ROWS)
    tn = _tile(d_model, MERGE_COLS)
    nj = d_model // tn
    o_spec = pl.BlockSpec((tm, bw), lambda i, j: (i, 0))
    g_specs = [pl.BlockSpec((tm, tn), lambda i, j, b=b: (i, b * nj + j)) for b in range(N_BRANCH)]
    return pl.pallas_call(
        _merge_kernel,
        grid=(t // tm, nj),
        in_specs=[o_spec, o_spec, o_spec,
                  pl.BlockSpec((N_BRANCH, bw, tn), lambda i, j: (0, 0, j))] + g_specs,
        out_specs=pl.BlockSpec((tm, tn), lambda i, j: (i, j)),
        out_shape=jax.ShapeDtypeStruct((t, d_model), BF16),
        compiler_params=_params("parallel", "parallel"),
        name="gated_merge",
    )(oa, ob, oc, w_branch, gates, gates, gates)


def _rope_tables(n_tok, rot_dim, scale):
    rows = n_tok // GRID_W
    row_idx = jnp.broadcast_to(jnp.arange(rows)[:, None], (rows, GRID_W)).reshape(-1).astype(F32)
    col_idx = jnp.broadcast_to(jnp.arange(GRID_W)[None, :], (rows, GRID_W)).reshape(-1).astype(F32)
    nq = rot_dim // 4
    freqs = ROPE_THETA ** (-(2.0 * jnp.arange(nq, dtype=F32)) / (rot_dim // 2))
    ang = jnp.stack([row_idx[:, None] * freqs, col_idx[:, None] * freqs], axis=1)
    cos, sin = jnp.cos(ang), jnp.sin(ang)
    pad = jnp.zeros((n_tok, LANES // 2 - 2 * nq), F32)
    cblk = jnp.concatenate([cos[:, 0], cos[:, 1], pad], axis=1)
    sblk = jnp.concatenate([sin[:, 0], sin[:, 1], pad], axis=1)
    c = jnp.concatenate([cblk, cblk], axis=1)
    s = jnp.concatenate([-sblk, sblk], axis=1)
    return c * scale, s * scale


def _rotary_lanes(w, rot_dim):
    nq = rot_dim // 4
    lead = w.shape[:-1]
    r = jnp.swapaxes(w.reshape(*lead, 2, 2, nq), -3, -2).reshape(*lead, 2, 2 * nq)
    r = jnp.pad(r, [(0, 0)] * (len(lead) + 1) + [(0, LANES // 2 - 2 * nq)])
    return r.reshape(*lead, LANES)


def _layer_weights(w_in, w_uq, w_ukv, g_qk_q, g_qk_k, q_lora, kv_lora, d_model, fw, gqa_scale):
    seg = [q_lora, kv_lora, MLA_ROPE, fw, GQA_HEADS * GQA_HEAD_DIM,
           GQA_KV_HEADS * GQA_HEAD_DIM, GQA_KV_HEADS * GQA_HEAD_DIM, N_BRANCH * d_model]
    offs = [0]
    for s in seg:
        offs.append(offs[-1] + s)
    cq, ckv, kpe, uf, qc, kc, vc, gl = [w_in[:, offs[i]:offs[i + 1]] for i in range(8)]
    w_gate = gl.astype(BF16)
    w_uv = jnp.concatenate([uf, vc], axis=1).astype(BF16)
    d_in = w_in.shape[0]
    qc = _rotary_lanes(qc.reshape(d_in, GQA_HEADS, GQA_HEAD_DIM), GQA_HEAD_DIM).reshape(d_in, -1)
    kc = _rotary_lanes(kc.reshape(d_in, GQA_KV_HEADS, GQA_HEAD_DIM), GQA_HEAD_DIM).reshape(d_in, -1)
    w_qk = jnp.concatenate([qc, kc], axis=1).astype(BF16)
    w_lat = jnp.concatenate([cq, ckv, _rotary_lanes(kpe, MLA_ROPE)], axis=1).astype(BF16)
    g_qk = jnp.concatenate([jnp.tile(_rotary_lanes(g_qk_q, GQA_HEAD_DIM) * gqa_scale, GQA_HEADS),
                            jnp.tile(_rotary_lanes(g_qk_k, GQA_HEAD_DIM), GQA_KV_HEADS)]).reshape(1, -1)
    uq = w_uq.reshape(q_lora, MLA_HEADS, MLA_NOPE + MLA_ROPE)
    uq = jnp.concatenate([uq[:, :, :MLA_NOPE], _rotary_lanes(uq[:, :, MLA_NOPE:], MLA_ROPE)], axis=2)
    w_uq_p = uq.reshape(q_lora, MLA_HEADS * MLA_QK_PAD).astype(BF16)
    ukv = w_ukv.reshape(kv_lora, MLA_HEADS, MLA_NOPE + MLA_V)
    w_ukv_p = jnp.concatenate([ukv[:, :, :MLA_NOPE].reshape(kv_lora, -1),
                               ukv[:, :, MLA_NOPE:].reshape(kv_lora, -1)], axis=1).astype(BF16)
    return w_gate, w_uv, w_qk, w_lat, g_qk, w_uq_p, w_ukv_p


def _layer(x, n_batch, n_tok, tabs, fnet_tabs, w_in, g_attn, g_qa, w_uq, g_kva, w_ukv,
           g_qk_q, g_qk_k, g_mlp, fw, own, gate_casts, up_casts):
    joined = isinstance(x, tuple)
    d_model = g_attn.shape[0]
    q_lora, kv_lora = g_qa.shape[0], g_kva.shape[0]
    gd = fw // FNET_GROUPS
    mla_scale = (MLA_NOPE + MLA_ROPE) ** -0.5 * LOG2E
    gqa_scale = GQA_HEAD_DIM ** -0.5 * LOG2E
    (ca, sa), (ca_s, sa_s), (cc, sc), ones_bd = tabs
    w_gate, w_uv, w_qk, w_lat, g_qk, w_uq_p, w_ukv_p = _layer_weights(
        w_in, w_uq, w_ukv, g_qk_q, g_qk_k, q_lora, kv_lora, d_model, fw, gqa_scale)

    tm = _tile(n_tok, 1024)
    nrb = n_tok // tm
    tab_map = lambda i, j: (i % nrb, 0)

    if joined:
        h = _rmsnorm_join(x[0], x[1], g_attn, BF16, "rmsnorm_attn")
    else:
        h = _rmsnorm(x, g_attn, BF16, "rmsnorm_attn")

    gates, *cast_out = _mm(h, w_gate, _epi_cast, [(w_gate.shape[1], _tile(w_gate.shape[1], 1024), BF16)],
                           tm=tm, tn=1024, casts=gate_casts, name="in_proj_gates")
    if own is None:
        wb2d, wo_bf, wup_bf, wdown_bf = cast_out
        wb_bf = wb2d.reshape(N_BRANCH, fw, d_model)
    else:
        wb_bf, wo_bf, wup_bf, wdown_bf = own
    (uv,) = _mm(h, w_uv, _epi_cast, [(w_uv.shape[1], _tile(w_uv.shape[1], 1280), BF16)],
                tm=tm, tn=1280, name="in_proj_uv")
    qk_tn = _tile(w_qk.shape[1], 512)
    (qk,) = _mm(h, w_qk, _epi_gqa_heads, [(w_qk.shape[1], qk_tn, BF16)],
                extras=[(g_qk, (1, qk_tn), lambda i, j: (0, j)),
                        (cc, (tm, LANES), tab_map), (sc, (tm, LANES), tab_map),
                        (ones_bd, ones_bd.shape, lambda i, j: (0, 0))],
                tm=tm, tn=512, name="in_proj_gqa_qk")
    cqn, ckvn, kpe = _mm(
        h, w_lat, functools.partial(_epi_latent, q_lora=q_lora, kv_lora=kv_lora),
        [(q_lora, q_lora, BF16), (kv_lora, kv_lora, BF16), (LANES, LANES, BF16)],
        extras=[(g_qa.reshape(1, -1), (1, q_lora), lambda i, j: (0, 0)),
                (g_kva.reshape(1, -1), (1, kv_lora), lambda i, j: (0, 0)),
                (ca, (tm, LANES), tab_map), (sa, (tm, LANES), tab_map)],
        tm=tm, tn=w_lat.shape[1], resident_w=True,
        chunks=[(0, q_lora), (q_lora, kv_lora), (q_lora + kv_lora, LANES)], name="in_proj_latent")

    (q_a,) = _mm(cqn, w_uq_p, functools.partial(_epi_mla_q, scale=mla_scale),
                 [(w_uq_p.shape[1], w_uq_p.shape[1], BF16)],
                 extras=[(ca_s, (tm, LANES), tab_map), (sa_s, (tm, LANES), tab_map)],
                 tm=tm, tn=w_uq_p.shape[1], chunk=2 * MLA_QK_PAD, resident_w=True, name="mla_q_up")
    (kv_a,) = _mm(ckvn, w_ukv_p, _epi_cast, [(w_ukv_p.shape[1], w_ukv_p.shape[1], BF16)],
                  tm=tm, tn=w_ukv_p.shape[1], chunk=1024, resident_w=True, name="mla_kv_up")
    o_a = _attention(q_a, MLA_QK_PAD,
                     [(kv_a, True, lambda hs: hs), (kpe, False, lambda hs: 0)],
                     kv_a, (True, lambda hs: MLA_HEADS // MLA_HEADS_PER_STEP + hs),
                     n_batch=n_batch, n_tok=n_tok, n_heads=MLA_HEADS, dv=MLA_V, hps=MLA_HEADS_PER_STEP,
                     row_chunk=MLA_ROW_CHUNK, name="mla_attention")

    o_b = _fnet(uv, 0, fnet_tabs, n_batch=n_batch, n_tok=n_tok, gd=gd)

    group = GQA_HEADS // GQA_KV_HEADS
    hps = GQA_HEADS_PER_STEP
    assert group % hps == 0
    o_c = _attention(qk, GQA_HEAD_DIM,
                     [(qk, False, lambda hs: GQA_HEADS + (hs * hps) // group)],
                     uv, (False, lambda hs: fw // GQA_HEAD_DIM + (hs * hps) // group),
                     n_batch=n_batch, n_tok=n_tok, n_heads=GQA_HEADS, dv=GQA_HEAD_DIM, hps=hps,
                     row_chunk=GQA_ROW_CHUNK, name="gqa_attention")

    merged = _merge(o_a, o_b, o_c, wb_bf, gates, d_model)
    otn = _tile(d_model, 512)
    if joined:
        nb_a = x[0].shape[0] // tm
        res_extras = [(x[0], (tm, otn), lambda i, j: (jnp.minimum(i, nb_a - 1), j)),
                      (x[1], (tm, otn), lambda i, j: (jnp.maximum(i - nb_a, 0), j))]
        res_epi = functools.partial(_epi_residual_join, nb_a=nb_a)
    else:
        res_extras = [(x, (tm, otn), lambda i, j: (i, j))]
        res_epi = _epi_residual
    (x,) = _mm(merged, wo_bf, res_epi, [(d_model, otn, F32)], extras=res_extras,
               tm=tm, tn=512, name="out_proj")

    h2 = _rmsnorm(x, g_mlp, BF16, "rmsnorm_mlp")
    d_ff = wup_bf.shape[1]
    act, *next_weights = _mm(h2, wup_bf, _epi_relu2, [(d_ff, _tile(d_ff, 1024), BF16)],
                             tm=tm, tn=1024, casts=up_casts, name="mlp_up")
    (x,) = _mm(act, wdown_bf, _epi_residual, [(d_model, _tile(d_model, 1024), F32)],
               extras=[(x, (tm, _tile(d_model, 1024)), lambda i, j: (i, j))],
               tm=tm, tn=1024, tk=4096, name="mlp_down")
    return x, next_weights


def kernel(x_prompt, x_sample, w_in, g_attn, g_qa, w_uq, g_kva, w_ukv, g_qk_q, g_qk_k,
           w_branch, w_o, g_mlp, w_up, w_down, g_final):
    n_tok, d_model = x_prompt.shape[1], x_prompt.shape[2]
    assert x_sample.shape[1:] == (n_tok, d_model)
    b_p, b_s = x_prompt.shape[0], x_sample.shape[0]
    n_batch = b_p + b_s
    x = (x_prompt.reshape(b_p * n_tok, d_model), x_sample.reshape(b_s * n_tok, d_model))

    mla_scale = (MLA_NOPE + MLA_ROPE) ** -0.5 * LOG2E
    heads_per_chunk = MXU_COLS // GQA_HEAD_DIM
    ones_bd = jnp.kron(jnp.eye(heads_per_chunk, dtype=F32), jnp.ones((GQA_HEAD_DIM, GQA_HEAD_DIM), F32))
    ones_bd = jnp.tile(ones_bd, (2, 1)).astype(BF16)
    tabs = (_rope_tables(n_tok, MLA_ROPE, 1.0), _rope_tables(n_tok, MLA_ROPE, mla_scale),
            _rope_tables(n_tok, GQA_HEAD_DIM, 1.0), ones_bd)
    n_layers, _, fw, _ = w_branch.shape
    fnet_tabs = _fnet_tables(n_tok, fw // FNET_GROUPS)

    wb_rows = w_branch.reshape(n_layers, N_BRANCH * fw, d_model)
    layer_casts = lambda l: [(wb_rows, l), (w_o, l), (w_up, l), (w_down, l)]
    w_in_l, own, gate_casts = w_in[0], None, layer_casts(0)
    for l in range(n_layers):
        up_casts = [(w_in, l + 1)] + layer_casts(l + 1) if l + 1 < n_layers else []
        x, nxt = _layer(x, n_batch, n_tok, tabs, fnet_tabs, w_in_l, g_attn[l], g_qa[l], w_uq[l], g_kva[l],
                        w_ukv[l], g_qk_q[l], g_qk_k[l], g_mlp[l], fw, own, gate_casts, up_casts)
        if nxt:
            w_in_l, own, gate_casts = nxt[0], (nxt[1].reshape(N_BRANCH, fw, d_model), *nxt[2:]), []
    y_p, y_s = _rmsnorm_split(x, g_final, b_p * n_tok, F32, "rmsnorm_final")
    return (y_p.reshape(b_p, n_tok, d_model), y_s.reshape(b_s, n_tok, d_model))
```

```python
import functools
import math

import jax
import jax.numpy as jnp
from jax import lax
from jax.experimental import pallas as pl
from jax.experimental.pallas import tpu as pltpu

F32 = jnp.float32
BF16 = jnp.bfloat16

GRID_W = 64
ROPE_THETA = 10000.0
EPS = 1e-6
MLA_HEADS = 16
MLA_NOPE = 128
MLA_ROPE = 64
MLA_V = 128
FNET_GROUPS = 4
GQA_HEADS = 16
GQA_KV_HEADS = 4
GQA_HEAD_DIM = 128
N_BRANCH = 3

LANES = 128
BF16_SUBLANES = 16
MXU_COLS = 256
MLA_QK_PAD = 2 * LANES
VMEM_LIMIT_BYTES = 60 * 2 ** 20
LOG2E = math.log2(math.e)
ATTN_Q_BLOCK = 2048
MLA_ROW_CHUNK = 256
GQA_ROW_CHUNK = 128
MLA_HEADS_PER_STEP = 4
GQA_HEADS_PER_STEP = 4
MERGE_ROWS = 1024
MERGE_COLS = 512


def _tile(n, pref):
    if n <= pref:
        return n
    t = (pref // LANES) * LANES
    while t > LANES and n % t:
        t -= LANES
    assert n % t == 0, (n, pref)
    return t


def _cast_rows(rows, steps):
    br = BF16_SUBLANES
    while rows % br or rows // br > steps:
        br += BF16_SUBLANES
    return br


def _params(*sem):
    return pltpu.CompilerParams(dimension_semantics=sem, vmem_limit_bytes=VMEM_LIMIT_BYTES)


def _rms(x, g):
    return (x * lax.rsqrt(jnp.mean(x * x, axis=-1, keepdims=True) + EPS)) * g


def _rmsnorm_kernel(x_ref, g_ref, o_ref):
    o_ref[...] = _rms(x_ref[...], g_ref[...]).astype(o_ref.dtype)


def _rmsnorm_join_kernel(xa_ref, xb_ref, g_ref, o_ref, *, nb_a):
    i = pl.program_id(0)

    @pl.when(i < nb_a)
    def _():
        o_ref[...] = _rms(xa_ref[...], g_ref[...]).astype(o_ref.dtype)

    @pl.when(i >= nb_a)
    def _():
        o_ref[...] = _rms(xb_ref[...], g_ref[...]).astype(o_ref.dtype)


def _rmsnorm_split_kernel(x_ref, g_ref, oa_ref, ob_ref, *, nb_a):
    i = pl.program_id(0)
    y = _rms(x_ref[...], g_ref[...]).astype(oa_ref.dtype)

    @pl.when(i < nb_a)
    def _():
        oa_ref[...] = y

    @pl.when(i >= nb_a)
    def _():
        ob_ref[...] = y


def _rmsnorm(x, g, out_dtype, name):
    t, d = x.shape
    tm = _tile(t, 256)
    return pl.pallas_call(
        _rmsnorm_kernel,
        grid=(t // tm,),
        in_specs=[pl.BlockSpec((tm, d), lambda i: (i, 0)),
                  pl.BlockSpec((1, d), lambda i: (0, 0))],
        out_specs=pl.BlockSpec((tm, d), lambda i: (i, 0)),
        out_shape=jax.ShapeDtypeStruct((t, d), out_dtype),
        compiler_params=_params("parallel"),
        name=name,
    )(x, g.reshape(1, d))


def _rmsnorm_join(xa, xb, g, out_dtype, name):
    (ta, d), tb = xa.shape, xb.shape[0]
    tm = _tile(math.gcd(ta, tb), 256)
    nb_a = ta // tm
    return pl.pallas_call(
        functools.partial(_rmsnorm_join_kernel, nb_a=nb_a),
        grid=((ta + tb) // tm,),
        in_specs=[pl.BlockSpec((tm, d), lambda i: (jnp.minimum(i, nb_a - 1), 0)),
                  pl.BlockSpec((tm, d), lambda i: (jnp.maximum(i - nb_a, 0), 0)),
                  pl.BlockSpec((1, d), lambda i: (0, 0))],
        out_specs=pl.BlockSpec((tm, d), lambda i: (i, 0)),
        out_shape=jax.ShapeDtypeStruct((ta + tb, d), out_dtype),
        compiler_params=_params("arbitrary"),
        name=name,
    )(xa, xb, g.reshape(1, d))


def _rmsnorm_split(x, g, ta, out_dtype, name):
    t, d = x.shape
    tm = _tile(math.gcd(ta, t - ta), 256)
    nb_a = ta // tm
    return pl.pallas_call(
        functools.partial(_rmsnorm_split_kernel, nb_a=nb_a),
        grid=(t // tm,),
        in_specs=[pl.BlockSpec((tm, d), lambda i: (i, 0)),
                  pl.BlockSpec((1, d), lambda i: (0, 0))],
        out_specs=[pl.BlockSpec((tm, d), lambda i: (jnp.minimum(i, nb_a - 1), 0)),
                   pl.BlockSpec((tm, d), lambda i: (jnp.maximum(i - nb_a, 0), 0))],
        out_shape=[jax.ShapeDtypeStruct((ta, d), out_dtype),
                   jax.ShapeDtypeStruct((t - ta, d), out_dtype)],
        compiler_params=_params("arbitrary"),
        name=name,
    )(x, g.reshape(1, d))


def _mm_kernel(a_ref, w_ref, *refs, epi, n_extra, n_cast, n_out, nk, chunks):
    extras = refs[:n_extra]
    cast_in = refs[n_extra:n_extra + n_cast]
    outs = refs[n_extra + n_cast:n_extra + n_cast + n_out]
    cast_out = refs[n_extra + n_cast + n_out:]
    for ci, co in zip(cast_in, cast_out):
        co[...] = ci[...].astype(co.dtype)
    if nk == 1:
        for c0, cw in chunks:
            acc = jnp.dot(a_ref[...], w_ref[:, c0:c0 + cw], preferred_element_type=F32)
            epi(acc, c0, extras, outs)
        return
    assert epi is _epi_residual
    k = pl.program_id(2)
    d = jnp.dot(a_ref[...], w_ref[...], preferred_element_type=F32)

    @pl.when(k == 0)
    def _():
        outs[0][...] = extras[0][...] + d

    @pl.when(k > 0)
    def _():
        outs[0][...] += d


def _mm(a, w, epi, outs, extras=(), *, tm, tn, tk=None, chunk=None, chunks=None, layer=None,
        resident_w=False, casts=(), name):
    m, kdim = a.shape
    n = w.shape[-1]
    tm = _tile(m, tm)
    tn = _tile(n, tn)
    tk = kdim if tk is None else _tile(kdim, tk)
    nk = kdim // tk
    if chunks is None:
        cw = tn if chunk is None else min(chunk, tn)
        chunks = [(c0, cw) for c0 in range(0, tn, cw)]
    assert nk == 1 or len(chunks) == 1
    grid = (m // tm, n // tn, nk)
    w_mode = dict(pipeline_mode=pl.Buffered(1)) if resident_w else {}
    assert not resident_w or (n == tn and nk == 1)
    if w.ndim == 3:
        w_spec = pl.BlockSpec((None, tk, tn), lambda i, j, k: (layer, k, j), **w_mode)
    else:
        w_spec = pl.BlockSpec((tk, tn), lambda i, j, k: (k, j), **w_mode)
    in_specs = [pl.BlockSpec((tm, tk), lambda i, j, k: (i, k)), w_spec]
    for _, bshape, imap in extras:
        in_specs.append(pl.BlockSpec(bshape, lambda i, j, k, imap=imap: imap(i, j)))
    out_specs = [pl.BlockSpec((tm, bc), lambda i, j, k: (i, j)) for _, bc, _ in outs]
    out_shape = [jax.ShapeDtypeStruct((m, tc), dt) for tc, _, dt in outs]
    nj = n // tn
    for arr, lyr in casts:
        rows, cols = arr.shape[-2:]
        br = _cast_rows(rows, grid[0] * nj * nk)
        nb = rows // br
        blk = lambda i, j, k, nb=nb: jnp.minimum((i * nj + j) * nk + k, nb - 1)
        in_specs.append(pl.BlockSpec((None, br, cols), lambda i, j, k, lyr=lyr, blk=blk: (lyr, blk(i, j, k), 0)))
        out_specs.append(pl.BlockSpec((br, cols), lambda i, j, k, blk=blk: (blk(i, j, k), 0)))
        out_shape.append(jax.ShapeDtypeStruct((rows, cols), BF16))
    sem = ("arbitrary",) * 3 if casts else ("parallel", "parallel", "arbitrary")
    return pl.pallas_call(
        functools.partial(_mm_kernel, epi=epi, n_extra=len(extras), n_cast=len(casts), n_out=len(outs),
                          nk=nk, chunks=chunks),
        grid=grid,
        in_specs=in_specs,
        out_specs=out_specs,
        out_shape=out_shape,
        compiler_params=_params(*sem),
        name=name,
    )(a, w, *[e[0] for e in extras], *[c[0] for c in casts])


def _cols(c0, acc):
    return slice(c0, c0 + acc.shape[1])


def _epi_cast(acc, c0, extras, outs):
    outs[0][:, _cols(c0, acc)] = acc.astype(outs[0].dtype)


def _epi_relu2(acc, c0, extras, outs):
    r = jnp.maximum(acc, 0.0)
    outs[0][:, _cols(c0, acc)] = (r * r).astype(outs[0].dtype)


def _epi_residual(acc, c0, extras, outs):
    cs = _cols(c0, acc)
    outs[0][:, cs] = extras[0][:, cs] + acc


def _epi_residual_join(acc, c0, extras, outs, *, nb_a):
    cs = _cols(c0, acc)
    i = pl.program_id(0)

    @pl.when(i < nb_a)
    def _():
        outs[0][:, cs] = extras[0][:, cs] + acc

    @pl.when(i >= nb_a)
    def _():
        outs[0][:, cs] = extras[1][:, cs] + acc


def _rope(x, c, s):
    return x * c + pltpu.roll(x, LANES // 2, 1) * s


def _epi_latent(acc, c0, extras, outs, *, q_lora, kv_lora):
    gq, gkv, c, s = extras
    if c0 == 0:
        outs[0][...] = _rms(acc, gq[...]).astype(BF16)
    elif c0 == q_lora:
        outs[1][...] = _rms(acc, gkv[...]).astype(BF16)
    else:
        outs[2][...] = _rope(acc, c[...], s[...]).astype(BF16)


def _epi_gqa_heads(acc, c0, extras, outs):
    g, c, s, ones_bd = extras
    cv, sv = c[...], s[...]
    for g0 in range(0, acc.shape[1], MXU_COLS):
        x = acc[:, g0:g0 + MXU_COLS]
        x2 = x * x
        hi = x2.astype(BF16)
        lo = (x2 - hi.astype(F32)).astype(BF16)
        ss = jnp.dot(jnp.concatenate([hi, lo], axis=1), ones_bd[...], preferred_element_type=F32)
        y = (x * lax.rsqrt(ss * (1.0 / GQA_HEAD_DIM) + EPS)) * g[:, c0 + g0:c0 + g0 + MXU_COLS]
        for h0 in range(0, MXU_COLS, GQA_HEAD_DIM):
            osl = slice(c0 + g0 + h0, c0 + g0 + h0 + GQA_HEAD_DIM)
            outs[0][:, osl] = _rope(y[:, h0:h0 + GQA_HEAD_DIM], cv, sv).astype(BF16)


def _epi_mla_q(acc, c0, extras, outs, *, scale):
    c, s = extras
    cv, sv = c[...], s[...]
    for h in range(acc.shape[1] // MLA_QK_PAD):
        lo = slice(h * MLA_QK_PAD, h * MLA_QK_PAD + LANES)
        hi = slice(h * MLA_QK_PAD + LANES, (h + 1) * MLA_QK_PAD)
        olo = slice(c0 + lo.start, c0 + lo.stop)
        ohi = slice(c0 + hi.start, c0 + hi.stop)
        outs[0][:, olo] = (acc[:, lo] * scale).astype(BF16)
        outs[0][:, ohi] = _rope(acc[:, hi], cv, sv).astype(BF16)


def _attn_kernel(q_ref, *refs, n_k, k_per_head, v_per_head, hps, rows):
    k_refs, v_ref, o_ref = refs[:n_k], refs[n_k], refs[n_k + 1]
    qc = q_ref.shape[1] // hps
    dv = o_ref.shape[1] // hps

    def head_kv(hh):
        parts = [r[:, hh * LANES:(hh + 1) * LANES] if ph else r[...] for r, ph in zip(k_refs, k_per_head)]
        k = parts[0] if len(parts) == 1 else jnp.concatenate(parts, axis=-1)
        v = v_ref[:, hh * dv:(hh + 1) * dv] if v_per_head else v_ref[...]
        return k, jnp.concatenate([v, jnp.ones_like(v)], axis=-1)

    shared = not (any(k_per_head) or v_per_head)
    if shared:
        k, v1 = head_kv(0)
    for hh in range(hps):
        if not shared:
            k, v1 = head_kv(hh)
        for r0 in range(0, q_ref.shape[0], rows):
            q = q_ref[r0:r0 + rows, hh * qc:(hh + 1) * qc]
            s = lax.dot_general(q, k, (((1,), (1,)), ((), ())), preferred_element_type=F32)
            m = jnp.max(s, axis=-1, keepdims=True)
            p = jnp.exp2(s - m).astype(BF16)
            o = jnp.dot(p, v1, preferred_element_type=F32)
            o_ref[r0:r0 + rows, hh * dv:(hh + 1) * dv] = (o[:, :dv] / o[:, dv:]).astype(o_ref.dtype)


def _attention(q, q_cols, ks, v, v_spec, *, n_batch, n_tok, n_heads, dv, hps, row_chunk, name):
    tq = _tile(n_tok, ATTN_Q_BLOCK)
    nq = n_tok // tq
    rows = min(tq, row_chunk)
    assert n_heads % hps == 0
    in_specs = [pl.BlockSpec((tq, hps * q_cols), lambda b, h, i: (b * nq + i, h))]
    args = [q]
    for arr, per_head, colfn in ks:
        cols = hps * LANES if per_head else LANES
        in_specs.append(pl.BlockSpec((n_tok, cols), lambda b, h, i, colfn=colfn: (b, colfn(h))))
        args.append(arr)
    v_per_head, v_colfn = v_spec
    in_specs.append(pl.BlockSpec((n_tok, hps * dv if v_per_head else dv), lambda b, h, i: (b, v_colfn(h))))
    args.append(v)
    return pl.pallas_call(
        functools.partial(_attn_kernel, n_k=len(ks), k_per_head=tuple(k[1] for k in ks),
                          v_per_head=v_per_head, hps=hps, rows=rows),
        grid=(n_batch, n_heads // hps, nq),
        in_specs=in_specs,
        out_specs=pl.BlockSpec((tq, hps * dv), lambda b, h, i: (b * nq + i, h)),
        out_shape=jax.ShapeDtypeStruct((n_batch * n_tok, n_heads * dv), BF16),
        compiler_params=_params("parallel", "parallel", "parallel"),
        name=name,
    )(*args)


def _fnet_chan_kernel(u_ref, w_ref, yc_ref, ys_ref):
    gd = yc_ref.shape[1]
    y = jnp.dot(u_ref[...], w_ref[...], preferred_element_type=F32)
    yc_ref[...] = y[:, :gd].astype(BF16)
    ys_ref[...] = y[:, gd:].astype(BF16)


def _fnet_pos_kernel(cn_ref, sn_ref, yc_ref, ys_ref, o_ref):
    o = jnp.dot(cn_ref[...], yc_ref[...], preferred_element_type=F32)
    o = o + jnp.dot(sn_ref[...], ys_ref[...], preferred_element_type=F32)
    o_ref[...] = o.astype(o_ref.dtype)


def _dft_tables(n):
    idx = jnp.arange(n, dtype=jnp.int32)
    jk = (idx[:, None] * idx[None, :]) % n
    ang = jk.astype(F32) * (2.0 * math.pi / n)
    return jnp.cos(ang), jnp.sin(ang)


def _fnet_tables(n_tok, gd):
    norm = 1.0 / math.sqrt(n_tok * gd)
    s_chan = 2.0 ** round(math.log2(norm) / 2)
    s_pos = norm / s_chan
    cc, sc = _dft_tables(gd)
    w_chan = (jnp.concatenate([cc, sc], axis=1) * s_chan).astype(BF16)
    cn, sn = _dft_tables(n_tok)
    return w_chan, (cn * s_pos).astype(BF16), (sn * (-s_pos)).astype(BF16)


def _fnet(z, u_col0, tables, *, n_batch, n_tok, gd):
    t = z.shape[0]
    width = FNET_GROUPS * gd
    w_chan, cn, msn = tables
    tm = _tile(t, 2048)
    ublk = u_col0 // gd
    assert u_col0 % gd == 0
    yc, ys = pl.pallas_call(
        _fnet_chan_kernel,
        grid=(t // tm, FNET_GROUPS),
        in_specs=[pl.BlockSpec((tm, gd), lambda i, g: (i, ublk + g)),
                  pl.BlockSpec((gd, 2 * gd), lambda i, g: (0, 0))],
        out_specs=[pl.BlockSpec((tm, gd), lambda i, g: (i, g))] * 2,
        out_shape=[jax.ShapeDtypeStruct((t, width), BF16)] * 2,
        compiler_params=_params("parallel", "parallel"),
        name="fnet_chan",
    )(z, w_chan)

    tp = _tile(n_tok, 1024)
    tn = _tile(width, 1024)
    npb = n_tok // tp
    return pl.pallas_call(
        _fnet_pos_kernel,
        grid=(n_batch, width // tn, npb),
        in_specs=[pl.BlockSpec((tp, n_tok), lambda b, j, i: (i, 0)),
                  pl.BlockSpec((tp, n_tok), lambda b, j, i: (i, 0)),
                  pl.BlockSpec((n_tok, tn), lambda b, j, i: (b, j)),
                  pl.BlockSpec((n_tok, tn), lambda b, j, i: (b, j))],
        out_specs=pl.BlockSpec((tp, tn), lambda b, j, i: (b * npb + i, j)),
        out_shape=jax.ShapeDtypeStruct((t, width), BF16),
        compiler_params=_params("parallel", "parallel", "parallel"),
        name="fnet_pos",
    )(cn, msn, yc, ys)


def _merge_kernel(oa_ref, ob_ref, oc_ref, w_ref, ga_ref, gb_ref, gc_ref, out_ref):
    tn = out_ref.shape[1]
    cw = min(tn, MXU_COLS)
    for c0 in range(0, tn, cw):
        cs = slice(c0, c0 + cw)
        acc = None
        for b, (o_ref, g_ref) in enumerate(((oa_ref, ga_ref), (ob_ref, gb_ref), (oc_ref, gc_ref))):
            gate = 1.0 / (1.0 + jnp.exp(-g_ref[:, cs].astype(F32)))
            c = gate * jnp.dot(o_ref[...], w_ref[b, :, cs], preferred_element_type=F32)
            acc = c if acc is None else acc + c
        out_ref[:, cs] = acc.astype(out_ref.dtype)


def _merge(oa, ob, oc, w_branch, gates, d_model):
    t, bw = oa.shape
    tm = _tile(t, MERGE_ROWS)
    tn = _tile(d_model, MERGE_COLS)
    nj = d_model // tn
    o_spec = pl.BlockSpec((tm, bw), lambda i, j: (i, 0))
    g_specs = [pl.BlockSpec((tm, tn), lambda i, j, b=b: (i, b * nj + j)) for b in range(N_BRANCH)]
    return pl.pallas_call(
        _merge_kernel,
        grid=(t // tm, nj),
        in_specs=[o_spec, o_spec, o_spec,
                  pl.BlockSpec((N_BRANCH, bw, tn), lambda i, j: (0, 0, j))] + g_specs,
        out_specs=pl.BlockSpec((tm, tn), lambda i, j: (i, j)),
        out_shape=jax.ShapeDtypeStruct((t, d_model), BF16),
        compiler_params=_params("parallel", "parallel"),
        name="gated_merge",
    )(oa, ob, oc, w_branch, gates, gates, gates)


def _rope_tables(n_tok, rot_dim, scale):
    rows = n_tok // GRID_W
    row_idx = jnp.broadcast_to(jnp.arange(rows)[:, None], (rows, GRID_W)).reshape(-1).astype(F32)
    col_idx = jnp.broadcast_to(jnp.arange(GRID_W)[None, :], (rows, GRID_W)).reshape(-1).astype(F32)
    nq = rot_dim // 4
    freqs = ROPE_THETA ** (-(2.0 * jnp.arange(nq, dtype=F32)) / (rot_dim // 2))
    ang = jnp.stack([row_idx[:, None] * freqs, col_idx[:, None] * freqs], axis=1)
    cos, sin = jnp.cos(ang), jnp.sin(ang)
    pad = jnp.zeros((n_tok, LANES // 2 - 2 * nq), F32)
    cblk = jnp.concatenate([cos[:, 0], cos[:, 1], pad], axis=1)
    sblk = jnp.concatenate([sin[:, 0], sin[:, 1], pad], axis=1)
    c = jnp.concatenate([cblk, cblk], axis=1)
    s = jnp.concatenate([-sblk, sblk], axis=1)
    return c * scale, s * scale


def _rotary_lanes(w, rot_dim):
    nq = rot_dim // 4
    lead = w.shape[:-1]
    r = jnp.swapaxes(w.reshape(*lead, 2, 2, nq), -3, -2).reshape(*lead, 2, 2 * nq)
    r = jnp.pad(r, [(0, 0)] * (len(lead) + 1) + [(0, LANES // 2 - 2 * nq)])
    return r.reshape(*lead, LANES)


def _layer_weights(w_in, w_uq, w_ukv, g_qk_q, g_qk_k, q_lora, kv_lora, d_model, fw, gqa_scale):
    seg = [q_lora, kv_lora, MLA_ROPE, fw, GQA_HEADS * GQA_HEAD_DIM,
           GQA_KV_HEADS * GQA_HEAD_DIM, GQA_KV_HEADS * GQA_HEAD_DIM, N_BRANCH * d_model]
    offs = [0]
    for s in seg:
        offs.append(offs[-1] + s)
    cq, ckv, kpe, uf, qc, kc, vc, gl = [w_in[:, offs[i]:offs[i + 1]] for i in range(8)]
    w_gate = gl.astype(BF16)
    w_uv = jnp.concatenate([uf, vc], axis=1).astype(BF16)
    d_in = w_in.shape[0]
    qc = _rotary_lanes(qc.reshape(d_in, GQA_HEADS, GQA_HEAD_DIM), GQA_HEAD_DIM).reshape(d_in, -1)
    kc = _rotary_lanes(kc.reshape(d_in, GQA_KV_HEADS, GQA_HEAD_DIM), GQA_HEAD_DIM).reshape(d_in, -1)
    w_qk = jnp.concatenate([qc, kc], axis=1).astype(BF16)
    w_lat = jnp.concatenate([cq, ckv, _rotary_lanes(kpe, MLA_ROPE)], axis=1).astype(BF16)
    g_qk = jnp.concatenate([jnp.tile(_rotary_lanes(g_qk_q, GQA_HEAD_DIM) * gqa_scale, GQA_HEADS),
                            jnp.tile(_rotary_lanes(g_qk_k, GQA_HEAD_DIM), GQA_KV_HEADS)]).reshape(1, -1)
    uq = w_uq.reshape(q_lora, MLA_HEADS, MLA_NOPE + MLA_ROPE)
    uq = jnp.concatenate([uq[:, :, :MLA_NOPE], _rotary_lanes(uq[:, :, MLA_NOPE:], MLA_ROPE)], axis=2)
    w_uq_p = uq.reshape(q_lora, MLA_HEADS * MLA_QK_PAD).astype(BF16)
    ukv = w_ukv.reshape(kv_lora, MLA_HEADS, MLA_NOPE + MLA_V)
    w_ukv_p = jnp.concatenate([ukv[:, :, :MLA_NOPE].reshape(kv_lora, -1),
                               ukv[:, :, MLA_NOPE:].reshape(kv_lora, -1)], axis=1).astype(BF16)
    return w_gate, w_uv, w_qk, w_lat, g_qk, w_uq_p, w_ukv_p


def _layer(x, n_batch, n_tok, tabs, fnet_tabs, w_in, g_attn, g_qa, w_uq, g_kva, w_ukv,
           g_qk_q, g_qk_k, g_mlp, fw, own, gate_casts, up_casts):
    joined = isinstance(x, tuple)
    d_model = g_attn.shape[0]
    q_lora, kv_lora = g_qa.shape[0], g_kva.shape[0]
    gd = fw // FNET_GROUPS
    mla_scale = (MLA_NOPE + MLA_ROPE) ** -0.5 * LOG2E
    gqa_scale = GQA_HEAD_DIM ** -0.5 * LOG2E
    (ca, sa), (ca_s, sa_s), (cc, sc), ones_bd = tabs
    w_gate, w_uv, w_qk, w_lat, g_qk, w_uq_p, w_ukv_p = _layer_weights(
        w_in, w_uq, w_ukv, g_qk_q, g_qk_k, q_lora, kv_lora, d_model, fw, gqa_scale)

    tm = _tile(n_tok, 1024)
    nrb = n_tok // tm
    tab_map = lambda i, j: (i % nrb, 0)

    if joined:
        h = _rmsnorm_join(x[0], x[1], g_attn, BF16, "rmsnorm_attn")
    else:
        h = _rmsnorm(x, g_attn, BF16, "rmsnorm_attn")

    gates, *cast_out = _mm(h, w_gate, _epi_cast, [(w_gate.shape[1], _tile(w_gate.shape[1], 1024), BF16)],
                           tm=tm, tn=1024, casts=gate_casts, name="in_proj_gates")
    if own is None:
        wb2d, wo_bf, wup_bf, wdown_bf = cast_out
        wb_bf = wb2d.reshape(N_BRANCH, fw, d_model)
    else:
        wb_bf, wo_bf, wup_bf, wdown_bf = own
    (uv,) = _mm(h, w_uv, _epi_cast, [(w_uv.shape[1], _tile(w_uv.shape[1], 1280), BF16)],
                tm=tm, tn=1280, name="in_proj_uv")
    qk_tn = _tile(w_qk.shape[1], 512)
    (qk,) = _mm(h, w_qk, _epi_gqa_heads, [(w_qk.shape[1], qk_tn, BF16)],
                extras=[(g_qk, (1, qk_tn), lambda i, j: (0, j)),
                        (cc, (tm, LANES), tab_map), (sc, (tm, LANES), tab_map),
                        (ones_bd, ones_bd.shape, lambda i, j: (0, 0))],
                tm=tm, tn=512, name="in_proj_gqa_qk")
    cqn, ckvn, kpe = _mm(
        h, w_lat, functools.partial(_epi_latent, q_lora=q_lora, kv_lora=kv_lora),
        [(q_lora, q_lora, BF16), (kv_lora, kv_lora, BF16), (LANES, LANES, BF16)],
        extras=[(g_qa.reshape(1, -1), (1, q_lora), lambda i, j: (0, 0)),
                (g_kva.reshape(1, -1), (1, kv_lora), lambda i, j: (0, 0)),
                (ca, (tm, LANES), tab_map), (sa, (tm, LANES), tab_map)],
        tm=tm, tn=w_lat.shape[1], resident_w=True,
        chunks=[(0, q_lora), (q_lora, kv_lora), (q_lora + kv_lora, LANES)], name="in_proj_latent")

    (q_a,) = _mm(cqn, w_uq_p, functools.partial(_epi_mla_q, scale=mla_scale),
                 [(w_uq_p.shape[1], w_uq_p.shape[1], BF16)],
                 extras=[(ca_s, (tm, LANES), tab_map), (sa_s, (tm, LANES), tab_map)],
                 tm=tm, tn=w_uq_p.shape[1], chunk=2 * MLA_QK_PAD, resident_w=True, name="mla_q_up")
    (kv_a,) = _mm(ckvn, w_ukv_p, _epi_cast, [(w_ukv_p.shape[1], w_ukv_p.shape[1], BF16)],
                  tm=tm, tn=w_ukv_p.shape[1], chunk=1024, resident_w=True, name="mla_kv_up")
    o_a = _attention(q_a, MLA_QK_PAD,
                     [(kv_a, True, lambda hs: hs), (kpe, False, lambda hs: 0)],
                     kv_a, (True, lambda hs: MLA_HEADS // MLA_HEADS_PER_STEP + hs),
                     n_batch=n_batch, n_tok=n_tok, n_heads=MLA_HEADS, dv=MLA_V, hps=MLA_HEADS_PER_STEP,
                     row_chunk=MLA_ROW_CHUNK, name="mla_attention")

    o_b = _fnet(uv, 0, fnet_tabs, n_batch=n_batch, n_tok=n_tok, gd=gd)

    group = GQA_HEADS // GQA_KV_HEADS
    hps = GQA_HEADS_PER_STEP
    assert group % hps == 0
    o_c = _attention(qk, GQA_HEAD_DIM,
                     [(qk, False, lambda hs: GQA_HEADS + (hs * hps) // group)],
                     uv, (False, lambda hs: fw // GQA_HEAD_DIM + (hs * hps) // group),
                     n_batch=n_batch, n_tok=n_tok, n_heads=GQA_HEADS, dv=GQA_HEAD_DIM, hps=hps,
                     row_chunk=GQA_ROW_CHUNK, name="gqa_attention")

    merged = _merge(o_a, o_b, o_c, wb_bf, gates, d_model)
    otn = _tile(d_model, 512)
    if joined:
        nb_a = x[0].shape[0] // tm
        res_extras = [(x[0], (tm, otn), lambda i, j: (jnp.minimum(i, nb_a - 1), j)),
                      (x[1], (tm, otn), lambda i, j: (jnp.maximum(i - nb_a, 0), j))]
        res_epi = functools.partial(_epi_residual_join, nb_a=nb_a)
    else:
        res_extras = [(x, (tm, otn), lambda i, j: (i, j))]
        res_epi = _epi_residual
    (x,) = _mm(merged, wo_bf, res_epi, [(d_model, otn, F32)], extras=res_extras,
               tm=tm, tn=512, name="out_proj")

    h2 = _rmsnorm(x, g_mlp, BF16, "rmsnorm_mlp")
    d_ff = wup_bf.shape[1]
    act, *next_weights = _mm(h2, wup_bf, _epi_relu2, [(d_ff, _tile(d_ff, 1024), BF16)],
                             tm=tm, tn=1024, casts=up_casts, name="mlp_up")
    (x,) = _mm(act, wdown_bf, _epi_residual, [(d_model, _tile(d_model, 1024), F32)],
               extras=[(x, (tm, _tile(d_model, 1024)), lambda i, j: (i, j))],
               tm=tm, tn=1024, tk=4096, name="mlp_down")
    return x, next_weights


def kernel(x_prompt, x_sample, w_in, g_attn, g_qa, w_uq, g_kva, w_ukv, g_qk_q, g_qk_k,
           w_branch, w_o, g_mlp, w_up, w_down, g_final):
    n_tok, d_model = x_prompt.shape[1], x_prompt.shape[2]
    assert x_sample.shape[1:] == (n_tok, d_model)
    b_p, b_s = x_prompt.shape[0], x_sample.shape[0]
    n_batch = b_p + b_s
    x = (x_prompt.reshape(b_p * n_tok, d_model), x_sample.reshape(b_s * n_tok, d_model))

    mla_scale = (MLA_NOPE + MLA_ROPE) ** -0.5 * LOG2E
    heads_per_chunk = MXU_COLS // GQA_HEAD_DIM
    ones_bd = jnp.kron(jnp.eye(heads_per_chunk, dtype=F32), jnp.ones((GQA_HEAD_DIM, GQA_HEAD_DIM), F32))
    ones_bd = jnp.tile(ones_bd, (2, 1)).astype(BF16)
    tabs = (_rope_tables(n_tok, MLA_ROPE, 1.0), _rope_tables(n_tok, MLA_ROPE, mla_scale),
            _rope_tables(n_tok, GQA_HEAD_DIM, 1.0), ones_bd)
    n_layers, _, fw, _ = w_branch.shape
    fnet_tabs = _fnet_tables(n_tok, fw // FNET_GROUPS)

    wb_rows = w_branch.reshape(n_layers, N_BRANCH * fw, d_model)
    layer_casts = lambda l: [(wb_rows, l), (w_o, l), (w_up, l), (w_down, l)]
    own, gate_casts = None, layer_casts(0)
    for l in range(n_layers):
        up_casts = layer_casts(l + 1) if l + 1 < n_layers else []
        x, nxt = _layer(x, n_batch, n_tok, tabs, fnet_tabs, w_in[l], g_attn[l], g_qa[l], w_uq[l], g_kva[l],
                        w_ukv[l], g_qk_q[l], g_qk_k[l], g_mlp[l], fw, own, gate_casts, up_casts)
        if nxt:
            own, gate_casts = (nxt[0].reshape(N_BRANCH, fw, d_model), *nxt[1:]), []
    y_p, y_s = _rmsnorm_split(x, g_final, b_p * n_tok, F32, "rmsnorm_final")
    return (y_p.reshape(b_p, n_tok, d_model), y_s.reshape(b_s, n_tok, d_model))
```

```python
import functools
import math

import jax
import jax.numpy as jnp
from jax import lax
from jax.experimental import pallas as pl
from jax.experimental.pallas import tpu as pltpu

F32 = jnp.float32
BF16 = jnp.bfloat16

GRID_W = 64
ROPE_THETA = 10000.0
EPS = 1e-6
MLA_HEADS = 16
MLA_NOPE = 128
MLA_ROPE = 64
MLA_V = 128
FNET_GROUPS = 4
GQA_HEADS = 16
GQA_KV_HEADS = 4
GQA_HEAD_DIM = 128
N_BRANCH = 3

LANES = 128
BF16_SUBLANES = 16
MXU_COLS = 256
MLA_QK_PAD = 2 * LANES
VMEM_LIMIT_BYTES = 60 * 2 ** 20
LOG2E = math.log2(math.e)
ATTN_Q_BLOCK = 2048
MLA_ROW_CHUNK = 256
GQA_ROW_CHUNK = 128
MLA_HEADS_PER_STEP = 4
GQA_HEADS_PER_STEP = 4
MERGE_ROWS = 1024
MERGE_COLS = 512


def _tile(n, pref):
    if n <= pref:
        return n
    t = (pref // LANES) * LANES
    while t > LANES and n % t:
        t -= LANES
    assert n % t == 0, (n, pref)
    return t


def _cast_rows(rows, steps):
    br = BF16_SUBLANES
    while rows % br or rows // br > steps:
        br += BF16_SUBLANES
    return br


def _params(*sem):
    return pltpu.CompilerParams(dimension_semantics=sem, vmem_limit_bytes=VMEM_LIMIT_BYTES)


def _rms(x, g):
    return (x * lax.rsqrt(jnp.mean(x * x, axis=-1, keepdims=True) + EPS)) * g


def _rmsnorm_kernel(x_ref, g_ref, o_ref):
    o_ref[...] = _rms(x_ref[...], g_ref[...]).astype(o_ref.dtype)


def _rmsnorm_join_kernel(xa_ref, xb_ref, g_ref, o_ref, *, nb_a):
    i = pl.program_id(0)

    @pl.when(i < nb_a)
    def _():
        o_ref[...] = _rms(xa_ref[...], g_ref[...]).astype(o_ref.dtype)

    @pl.when(i >= nb_a)
    def _():
        o_ref[...] = _rms(xb_ref[...], g_ref[...]).astype(o_ref.dtype)


def _rmsnorm_split_kernel(x_ref, g_ref, oa_ref, ob_ref, *, nb_a):
    i = pl.program_id(0)
    y = _rms(x_ref[...], g_ref[...]).astype(oa_ref.dtype)

    @pl.when(i < nb_a)
    def _():
        oa_ref[...] = y

    @pl.when(i >= nb_a)
    def _():
        ob_ref[...] = y


def _rmsnorm(x, g, out_dtype, name):
    t, d = x.shape
    tm = _tile(t, 256)
    return pl.pallas_call(
        _rmsnorm_kernel,
        grid=(t // tm,),
        in_specs=[pl.BlockSpec((tm, d), lambda i: (i, 0)),
                  pl.BlockSpec((1, d), lambda i: (0, 0))],
        out_specs=pl.BlockSpec((tm, d), lambda i: (i, 0)),
        out_shape=jax.ShapeDtypeStruct((t, d), out_dtype),
        compiler_params=_params("parallel"),
        name=name,
    )(x, g.reshape(1, d))


def _rmsnorm_join(xa, xb, g, out_dtype, name):
    (ta, d), tb = xa.shape, xb.shape[0]
    tm = _tile(math.gcd(ta, tb), 256)
    nb_a = ta // tm
    return pl.pallas_call(
        functools.partial(_rmsnorm_join_kernel, nb_a=nb_a),
        grid=((ta + tb) // tm,),
        in_specs=[pl.BlockSpec((tm, d), lambda i: (jnp.minimum(i, nb_a - 1), 0)),
                  pl.BlockSpec((tm, d), lambda i: (jnp.maximum(i - nb_a, 0), 0)),
                  pl.BlockSpec((1, d), lambda i: (0, 0))],
        out_specs=pl.BlockSpec((tm, d), lambda i: (i, 0)),
        out_shape=jax.ShapeDtypeStruct((ta + tb, d), out_dtype),
        compiler_params=_params("arbitrary"),
        name=name,
    )(xa, xb, g.reshape(1, d))


def _rmsnorm_split(x, g, ta, out_dtype, name):
    t, d = x.shape
    tm = _tile(math.gcd(ta, t - ta), 256)
    nb_a = ta // tm
    return pl.pallas_call(
        functools.partial(_rmsnorm_split_kernel, nb_a=nb_a),
        grid=(t // tm,),
        in_specs=[pl.BlockSpec((tm, d), lambda i: (i, 0)),
                  pl.BlockSpec((1, d), lambda i: (0, 0))],
        out_specs=[pl.BlockSpec((tm, d), lambda i: (jnp.minimum(i, nb_a - 1), 0)),
                   pl.BlockSpec((tm, d), lambda i: (jnp.maximum(i - nb_a, 0), 0))],
        out_shape=[jax.ShapeDtypeStruct((ta, d), out_dtype),
                   jax.ShapeDtypeStruct((t - ta, d), out_dtype)],
        compiler_params=_params("arbitrary"),
        name=name,
    )(x, g.reshape(1, d))


def _mm_kernel(a_ref, w_ref, *refs, epi, n_extra, n_cast, n_out, nk, chunks):
    extras = refs[:n_extra]
    cast_in = refs[n_extra:n_extra + n_cast]
    outs = refs[n_extra + n_cast:n_extra + n_cast + n_out]
    cast_out = refs[n_extra + n_cast + n_out:]
    for ci, co in zip(cast_in, cast_out):
        co[...] = ci[...].astype(co.dtype)
    if nk == 1:
        for c0, cw in chunks:
            acc = jnp.dot(a_ref[...], w_ref[:, c0:c0 + cw], preferred_element_type=F32)
            epi(acc, c0, extras, outs)
        return
    assert epi in (_epi_residual, _epi_residual_emit)
    k = pl.program_id(2)
    d = jnp.dot(a_ref[...], w_ref[...], preferred_element_type=F32)

    @pl.when(k == 0)
    def _():
        outs[0][...] = extras[0][...] + d

    @pl.when(k > 0)
    def _():
        outs[0][...] += d

    if epi is _epi_residual_emit:
        j = pl.program_id(1)

        def emit():
            x = outs[0][...]
            outs[1][...] = (x * extras[1][...]).astype(outs[1].dtype)
            return _row_sumsq(x)

        @pl.when((k == nk - 1) & (j == 0))
        def _():
            outs[2][...] = emit()

        @pl.when((k == nk - 1) & (j > 0))
        def _():
            outs[2][...] += emit()


def _mm(a, w, epi, outs, extras=(), *, tm, tn, tk=None, chunk=None, chunks=None, layer=None,
        resident_w=False, casts=(), row_ss=None, name):
    m, kdim = a.shape
    n = w.shape[-1]
    tm = _tile(m, tm)
    tn = _tile(n, tn)
    tk = kdim if tk is None else _tile(kdim, tk)
    nk = kdim // tk
    if chunks is None:
        cw = tn if chunk is None else min(chunk, tn)
        chunks = [(c0, cw) for c0 in range(0, tn, cw)]
    assert nk == 1 or len(chunks) == 1
    grid = (m // tm, n // tn, nk)
    w_mode = dict(pipeline_mode=pl.Buffered(1)) if resident_w else {}
    assert not resident_w or (n == tn and nk == 1)
    if w.ndim == 3:
        w_spec = pl.BlockSpec((None, tk, tn), lambda i, j, k: (layer, k, j), **w_mode)
    else:
        w_spec = pl.BlockSpec((tk, tn), lambda i, j, k: (k, j), **w_mode)
    in_specs = [pl.BlockSpec((tm, tk), lambda i, j, k: (i, k)), w_spec]
    extras = list(extras)
    if row_ss is not None:
        extras.append((row_ss, (tm, LANES), lambda i, j: (i, 0)))
        epi = _row_scaled(epi, kdim)
    for _, bshape, imap in extras:
        in_specs.append(pl.BlockSpec(bshape, lambda i, j, k, imap=imap: imap(i, j)))
    out_specs = [pl.BlockSpec((tm, o[1]), (lambda i, j, k: (i, 0)) if o[3:] == ("row",) else
                              (lambda i, j, k: (i, j))) for o in outs]
    out_shape = [jax.ShapeDtypeStruct((m, o[0]), o[2]) for o in outs]
    nj = n // tn
    for arr, lyr in casts:
        rows, cols = arr.shape[-2:]
        br = _cast_rows(rows, grid[0] * nj * nk)
        nb = rows // br
        blk = lambda i, j, k, nb=nb: jnp.minimum((i * nj + j) * nk + k, nb - 1)
        in_specs.append(pl.BlockSpec((None, br, cols), lambda i, j, k, lyr=lyr, blk=blk: (lyr, blk(i, j, k), 0)))
        out_specs.append(pl.BlockSpec((br, cols), lambda i, j, k, blk=blk: (blk(i, j, k), 0)))
        out_shape.append(jax.ShapeDtypeStruct((rows, cols), BF16))
    revisits = casts or any(o[3:] == ("row",) for o in outs)
    sem = ("arbitrary",) * 3 if revisits else ("parallel", "parallel", "arbitrary")
    return pl.pallas_call(
        functools.partial(_mm_kernel, epi=epi, n_extra=len(extras), n_cast=len(casts), n_out=len(outs),
                          nk=nk, chunks=chunks),
        grid=grid,
        in_specs=in_specs,
        out_specs=out_specs,
        out_shape=out_shape,
        compiler_params=_params(*sem),
        name=name,
    )(a, w, *[e[0] for e in extras], *[c[0] for c in casts])


def _cols(c0, acc):
    return slice(c0, c0 + acc.shape[1])


def _epi_cast(acc, c0, extras, outs):
    outs[0][:, _cols(c0, acc)] = acc.astype(outs[0].dtype)


def _epi_relu2(acc, c0, extras, outs):
    r = jnp.maximum(acc, 0.0)
    outs[0][:, _cols(c0, acc)] = (r * r).astype(outs[0].dtype)


def _epi_residual(acc, c0, extras, outs):
    cs = _cols(c0, acc)
    outs[0][:, cs] = extras[0][:, cs] + acc


def _epi_residual_join(acc, c0, extras, outs, *, nb_a):
    cs = _cols(c0, acc)
    i = pl.program_id(0)

    @pl.when(i < nb_a)
    def _():
        outs[0][:, cs] = extras[0][:, cs] + acc

    @pl.when(i >= nb_a)
    def _():
        outs[0][:, cs] = extras[1][:, cs] + acc


def _row_sumsq(x):
    return jnp.broadcast_to(jnp.sum(x * x, axis=-1, keepdims=True), (x.shape[0], LANES))


def _epi_residual_emit(acc, c0, extras, outs, *, nb_a=None):
    assert c0 == 0 and acc.shape[1] == outs[0].shape[1]
    if nb_a is None:
        _epi_residual(acc, c0, extras, outs)
    else:
        _epi_residual_join(acc, c0, extras, outs, nb_a=nb_a)
    x = outs[0][...]
    outs[1][...] = (x * extras[-1][...]).astype(outs[1].dtype)
    part = _row_sumsq(x)
    j = pl.program_id(1)

    @pl.when(j == 0)
    def _():
        outs[2][...] = part

    @pl.when(j > 0)
    def _():
        outs[2][...] += part


def _row_scaled(epi, d):
    def wrapped(acc, c0, extras, outs):
        r = lax.rsqrt(extras[-1][...] * (1.0 / d) + EPS)
        epi(acc * jnp.tile(r, (1, acc.shape[1] // LANES)), c0, extras[:-1], outs)
    return wrapped


def _rope(x, c, s):
    return x * c + pltpu.roll(x, LANES // 2, 1) * s


def _epi_latent(acc, c0, extras, outs, *, q_lora, kv_lora):
    gq, gkv, c, s = extras
    if c0 == 0:
        outs[0][...] = _rms(acc, gq[...]).astype(BF16)
    elif c0 == q_lora:
        outs[1][...] = _rms(acc, gkv[...]).astype(BF16)
    else:
        outs[2][...] = _rope(acc, c[...], s[...]).astype(BF16)


def _epi_gqa_heads(acc, c0, extras, outs):
    g, c, s, ones_bd = extras
    cv, sv = c[...], s[...]
    for g0 in range(0, acc.shape[1], MXU_COLS):
        x = acc[:, g0:g0 + MXU_COLS]
        x2 = x * x
        hi = x2.astype(BF16)
        lo = (x2 - hi.astype(F32)).astype(BF16)
        ss = jnp.dot(jnp.concatenate([hi, lo], axis=1), ones_bd[...], preferred_element_type=F32)
        y = (x * lax.rsqrt(ss * (1.0 / GQA_HEAD_DIM) + EPS)) * g[:, c0 + g0:c0 + g0 + MXU_COLS]
        for h0 in range(0, MXU_COLS, GQA_HEAD_DIM):
            osl = slice(c0 + g0 + h0, c0 + g0 + h0 + GQA_HEAD_DIM)
            outs[0][:, osl] = _rope(y[:, h0:h0 + GQA_HEAD_DIM], cv, sv).astype(BF16)


def _epi_mla_q(acc, c0, extras, outs, *, scale):
    c, s = extras
    cv, sv = c[...], s[...]
    for h in range(acc.shape[1] // MLA_QK_PAD):
        lo = slice(h * MLA_QK_PAD, h * MLA_QK_PAD + LANES)
        hi = slice(h * MLA_QK_PAD + LANES, (h + 1) * MLA_QK_PAD)
        olo = slice(c0 + lo.start, c0 + lo.stop)
        ohi = slice(c0 + hi.start, c0 + hi.stop)
        outs[0][:, olo] = (acc[:, lo] * scale).astype(BF16)
        outs[0][:, ohi] = _rope(acc[:, hi], cv, sv).astype(BF16)


def _attn_kernel(q_ref, *refs, n_k, k_per_head, v_per_head, hps, rows):
    k_refs, v_ref, o_ref = refs[:n_k], refs[n_k], refs[n_k + 1]
    qc = q_ref.shape[1] // hps
    dv = o_ref.shape[1] // hps

    def head_kv(hh):
        parts = [r[:, hh * LANES:(hh + 1) * LANES] if ph else r[...] for r, ph in zip(k_refs, k_per_head)]
        k = parts[0] if len(parts) == 1 else jnp.concatenate(parts, axis=-1)
        v = v_ref[:, hh * dv:(hh + 1) * dv] if v_per_head else v_ref[...]
        return k, jnp.concatenate([v, jnp.ones_like(v)], axis=-1)

    shared = not (any(k_per_head) or v_per_head)
    if shared:
        k, v1 = head_kv(0)
    for hh in range(hps):
        if not shared:
            k, v1 = head_kv(hh)
        for r0 in range(0, q_ref.shape[0], rows):
            q = q_ref[r0:r0 + rows, hh * qc:(hh + 1) * qc]
            s = lax.dot_general(q, k, (((1,), (1,)), ((), ())), preferred_element_type=F32)
            m = jnp.max(s, axis=-1, keepdims=True)
            p = jnp.exp2(s - m).astype(BF16)
            o = jnp.dot(p, v1, preferred_element_type=F32)
            o_ref[r0:r0 + rows, hh * dv:(hh + 1) * dv] = (o[:, :dv] / o[:, dv:]).astype(o_ref.dtype)


def _attention(q, q_cols, ks, v, v_spec, *, n_batch, n_tok, n_heads, dv, hps, row_chunk, name):
    tq = _tile(n_tok, ATTN_Q_BLOCK)
    nq = n_tok // tq
    rows = min(tq, row_chunk)
    assert n_heads % hps == 0
    in_specs = [pl.BlockSpec((tq, hps * q_cols), lambda b, h, i: (b * nq + i, h))]
    args = [q]
    for arr, per_head, colfn in ks:
        cols = hps * LANES if per_head else LANES
        in_specs.append(pl.BlockSpec((n_tok, cols), lambda b, h, i, colfn=colfn: (b, colfn(h))))
        args.append(arr)
    v_per_head, v_colfn = v_spec
    in_specs.append(pl.BlockSpec((n_tok, hps * dv if v_per_head else dv), lambda b, h, i: (b, v_colfn(h))))
    args.append(v)
    return pl.pallas_call(
        functools.partial(_attn_kernel, n_k=len(ks), k_per_head=tuple(k[1] for k in ks),
                          v_per_head=v_per_head, hps=hps, rows=rows),
        grid=(n_batch, n_heads // hps, nq),
        in_specs=in_specs,
        out_specs=pl.BlockSpec((tq, hps * dv), lambda b, h, i: (b * nq + i, h)),
        out_shape=jax.ShapeDtypeStruct((n_batch * n_tok, n_heads * dv), BF16),
        compiler_params=_params("parallel", "parallel", "parallel"),
        name=name,
    )(*args)


def _fnet_chan_kernel(u_ref, w_ref, yc_ref, ys_ref):
    gd = yc_ref.shape[1]
    y = jnp.dot(u_ref[...], w_ref[...], preferred_element_type=F32)
    yc_ref[...] = y[:, :gd].astype(BF16)
    ys_ref[...] = y[:, gd:].astype(BF16)


def _fnet_pos_kernel(cn_ref, sn_ref, yc_ref, ys_ref, o_ref):
    o = jnp.dot(cn_ref[...], yc_ref[...], preferred_element_type=F32)
    o = o + jnp.dot(sn_ref[...], ys_ref[...], preferred_element_type=F32)
    o_ref[...] = o.astype(o_ref.dtype)


def _dft_tables(n):
    idx = jnp.arange(n, dtype=jnp.int32)
    jk = (idx[:, None] * idx[None, :]) % n
    ang = jk.astype(F32) * (2.0 * math.pi / n)
    return jnp.cos(ang), jnp.sin(ang)


def _fnet_tables(n_tok, gd):
    norm = 1.0 / math.sqrt(n_tok * gd)
    s_chan = 2.0 ** round(math.log2(norm) / 2)
    s_pos = norm / s_chan
    cc, sc = _dft_tables(gd)
    w_chan = (jnp.concatenate([cc, sc], axis=1) * s_chan).astype(BF16)
    cn, sn = _dft_tables(n_tok)
    return w_chan, (cn * s_pos).astype(BF16), (sn * (-s_pos)).astype(BF16)


def _fnet(z, u_col0, tables, *, n_batch, n_tok, gd):
    t = z.shape[0]
    width = FNET_GROUPS * gd
    w_chan, cn, msn = tables
    tm = _tile(t, 2048)
    ublk = u_col0 // gd
    assert u_col0 % gd == 0
    yc, ys = pl.pallas_call(
        _fnet_chan_kernel,
        grid=(t // tm, FNET_GROUPS),
        in_specs=[pl.BlockSpec((tm, gd), lambda i, g: (i, ublk + g)),
                  pl.BlockSpec((gd, 2 * gd), lambda i, g: (0, 0))],
        out_specs=[pl.BlockSpec((tm, gd), lambda i, g: (i, g))] * 2,
        out_shape=[jax.ShapeDtypeStruct((t, width), BF16)] * 2,
        compiler_params=_params("parallel", "parallel"),
        name="fnet_chan",
    )(z, w_chan)

    tp = _tile(n_tok, 1024)
    tn = _tile(width, 1024)
    npb = n_tok // tp
    return pl.pallas_call(
        _fnet_pos_kernel,
        grid=(n_batch, width // tn, npb),
        in_specs=[pl.BlockSpec((tp, n_tok), lambda b, j, i: (i, 0)),
                  pl.BlockSpec((tp, n_tok), lambda b, j, i: (i, 0)),
                  pl.BlockSpec((n_tok, tn), lambda b, j, i: (b, j)),
                  pl.BlockSpec((n_tok, tn), lambda b, j, i: (b, j))],
        out_specs=pl.BlockSpec((tp, tn), lambda b, j, i: (b * npb + i, j)),
        out_shape=jax.ShapeDtypeStruct((t, width), BF16),
        compiler_params=_params("parallel", "parallel", "parallel"),
        name="fnet_pos",
    )(cn, msn, yc, ys)


def _merge_kernel(oa_ref, ob_ref, oc_ref, w_ref, ga_ref, gb_ref, gc_ref, out_ref):
    tn = out_ref.shape[1]
    cw = min(tn, MXU_COLS)
    for c0 in range(0, tn, cw):
        cs = slice(c0, c0 + cw)
        acc = None
        for b, (o_ref, g_ref) in enumerate(((oa_ref, ga_ref), (ob_ref, gb_ref), (oc_ref, gc_ref))):
            gate = 1.0 / (1.0 + jnp.exp(-g_ref[:, cs].astype(F32)))
            c = gate * jnp.dot(o_ref[...], w_ref[b, :, cs], preferred_element_type=F32)
            acc = c if acc is None else acc + c
        out_ref[:, cs] = acc.astype(out_ref.dtype)


def _merge(oa, ob, oc, w_branch, gates, d_model):
    t, bw = oa.shape
    tm = _tile(t, MERGE_ROWS)
    tn = _tile(d_model, MERGE_COLS)
    nj = d_model // tn
    o_spec = pl.BlockSpec((tm, bw), lambda i, j: (i, 0))
    g_specs = [pl.BlockSpec((tm, tn), lambda i, j, b=b: (i, b * nj + j)) for b in range(N_BRANCH)]
    return pl.pallas_call(
        _merge_kernel,
        grid=(t // tm, nj),
        in_specs=[o_spec, o_spec, o_spec,
                  pl.BlockSpec((N_BRANCH, bw, tn), lambda i, j: (0, 0, j))] + g_specs,
        out_specs=pl.BlockSpec((tm, tn), lambda i, j: (i, j)),
        out_shape=jax.ShapeDtypeStruct((t, d_model), BF16),
        compiler_params=_params("parallel", "parallel"),
        name="gated_merge",
    )(oa, ob, oc, w_branch, gates, gates, gates)


def _rope_tables(n_tok, rot_dim, scale):
    rows = n_tok // GRID_W
    row_idx = jnp.broadcast_to(jnp.arange(rows)[:, None], (rows, GRID_W)).reshape(-1).astype(F32)
    col_idx = jnp.broadcast_to(jnp.arange(GRID_W)[None, :], (rows, GRID_W)).reshape(-1).astype(F32)
    nq = rot_dim // 4
    freqs = ROPE_THETA ** (-(2.0 * jnp.arange(nq, dtype=F32)) / (rot_dim // 2))
    ang = jnp.stack([row_idx[:, None] * freqs, col_idx[:, None] * freqs], axis=1)
    cos, sin = jnp.cos(ang), jnp.sin(ang)
    pad = jnp.zeros((n_tok, LANES // 2 - 2 * nq), F32)
    cblk = jnp.concatenate([cos[:, 0], cos[:, 1], pad], axis=1)
    sblk = jnp.concatenate([sin[:, 0], sin[:, 1], pad], axis=1)
    c = jnp.concatenate([cblk, cblk], axis=1)
    s = jnp.concatenate([-sblk, sblk], axis=1)
    return c * scale, s * scale


def _rotary_lanes(w, rot_dim):
    nq = rot_dim // 4
    lead = w.shape[:-1]
    r = jnp.swapaxes(w.reshape(*lead, 2, 2, nq), -3, -2).reshape(*lead, 2, 2 * nq)
    r = jnp.pad(r, [(0, 0)] * (len(lead) + 1) + [(0, LANES // 2 - 2 * nq)])
    return r.reshape(*lead, LANES)


def _layer_weights(w_in, w_uq, w_ukv, g_qk_q, g_qk_k, q_lora, kv_lora, d_model, fw, gqa_scale):
    seg = [q_lora, kv_lora, MLA_ROPE, fw, GQA_HEADS * GQA_HEAD_DIM,
           GQA_KV_HEADS * GQA_HEAD_DIM, GQA_KV_HEADS * GQA_HEAD_DIM, N_BRANCH * d_model]
    offs = [0]
    for s in seg:
        offs.append(offs[-1] + s)
    cq, ckv, kpe, uf, qc, kc, vc, gl = [w_in[:, offs[i]:offs[i + 1]] for i in range(8)]
    w_gate = gl.astype(BF16)
    w_uv = jnp.concatenate([uf, vc], axis=1).astype(BF16)
    d_in = w_in.shape[0]
    qc = _rotary_lanes(qc.reshape(d_in, GQA_HEADS, GQA_HEAD_DIM), GQA_HEAD_DIM).reshape(d_in, -1)
    kc = _rotary_lanes(kc.reshape(d_in, GQA_KV_HEADS, GQA_HEAD_DIM), GQA_HEAD_DIM).reshape(d_in, -1)
    w_qk = jnp.concatenate([qc, kc], axis=1).astype(BF16)
    w_lat = jnp.concatenate([cq, ckv, _rotary_lanes(kpe, MLA_ROPE)], axis=1).astype(BF16)
    g_qk = jnp.concatenate([jnp.tile(_rotary_lanes(g_qk_q, GQA_HEAD_DIM) * gqa_scale, GQA_HEADS),
                            jnp.tile(_rotary_lanes(g_qk_k, GQA_HEAD_DIM), GQA_KV_HEADS)]).reshape(1, -1)
    uq = w_uq.reshape(q_lora, MLA_HEADS, MLA_NOPE + MLA_ROPE)
    uq = jnp.concatenate([uq[:, :, :MLA_NOPE], _rotary_lanes(uq[:, :, MLA_NOPE:], MLA_ROPE)], axis=2)
    w_uq_p = uq.reshape(q_lora, MLA_HEADS * MLA_QK_PAD).astype(BF16)
    ukv = w_ukv.reshape(kv_lora, MLA_HEADS, MLA_NOPE + MLA_V)
    w_ukv_p = jnp.concatenate([ukv[:, :, :MLA_NOPE].reshape(kv_lora, -1),
                               ukv[:, :, MLA_NOPE:].reshape(kv_lora, -1)], axis=1).astype(BF16)
    return w_gate, w_uv, w_qk, w_lat, g_qk, w_uq_p, w_ukv_p


def _layer(x, h_in, n_batch, n_tok, tabs, fnet_tabs, w_in, g_attn, g_qa, w_uq, g_kva, w_ukv,
           g_qk_q, g_qk_k, g_mlp, g_next, fw, own, gate_casts, up_casts):
    joined = isinstance(x, tuple)
    d_model = g_attn.shape[0]
    q_lora, kv_lora = g_qa.shape[0], g_kva.shape[0]
    gd = fw // FNET_GROUPS
    mla_scale = (MLA_NOPE + MLA_ROPE) ** -0.5 * LOG2E
    gqa_scale = GQA_HEAD_DIM ** -0.5 * LOG2E
    (ca, sa), (ca_s, sa_s), (cc, sc), ones_bd = tabs
    w_gate, w_uv, w_qk, w_lat, g_qk, w_uq_p, w_ukv_p = _layer_weights(
        w_in, w_uq, w_ukv, g_qk_q, g_qk_k, q_lora, kv_lora, d_model, fw, gqa_scale)

    tm = _tile(n_tok, 1024)
    nrb = n_tok // tm
    tab_map = lambda i, j: (i % nrb, 0)

    ss_h = None
    if h_in is not None:
        h, ss_h = h_in
    elif joined:
        h = _rmsnorm_join(x[0], x[1], g_attn, BF16, "rmsnorm_attn")
    else:
        h = _rmsnorm(x, g_attn, BF16, "rmsnorm_attn")

    gates, *cast_out = _mm(h, w_gate, _epi_cast, [(w_gate.shape[1], _tile(w_gate.shape[1], 1024), BF16)],
                           tm=tm, tn=1024, casts=gate_casts, row_ss=ss_h, name="in_proj_gates")
    if own is None:
        wb2d, wo_bf, wup_bf, wdown_bf = cast_out
        wb_bf = wb2d.reshape(N_BRANCH, fw, d_model)
    else:
        wb_bf, wo_bf, wup_bf, wdown_bf = own
    (uv,) = _mm(h, w_uv, _epi_cast, [(w_uv.shape[1], _tile(w_uv.shape[1], 1280), BF16)],
                tm=tm, tn=1280, row_ss=ss_h, name="in_proj_uv")
    qk_tn = _tile(w_qk.shape[1], 512)
    (qk,) = _mm(h, w_qk, _epi_gqa_heads, [(w_qk.shape[1], qk_tn, BF16)],
                extras=[(g_qk, (1, qk_tn), lambda i, j: (0, j)),
                        (cc, (tm, LANES), tab_map), (sc, (tm, LANES), tab_map),
                        (ones_bd, ones_bd.shape, lambda i, j: (0, 0))],
                tm=tm, tn=512, row_ss=ss_h, name="in_proj_gqa_qk")
    cqn, ckvn, kpe = _mm(
        h, w_lat, functools.partial(_epi_latent, q_lora=q_lora, kv_lora=kv_lora),
        [(q_lora, q_lora, BF16), (kv_lora, kv_lora, BF16), (LANES, LANES, BF16)],
        extras=[(g_qa.reshape(1, -1), (1, q_lora), lambda i, j: (0, 0)),
                (g_kva.reshape(1, -1), (1, kv_lora), lambda i, j: (0, 0)),
                (ca, (tm, LANES), tab_map), (sa, (tm, LANES), tab_map)],
        tm=tm, tn=w_lat.shape[1], resident_w=True, row_ss=ss_h,
        chunks=[(0, q_lora), (q_lora, kv_lora), (q_lora + kv_lora, LANES)], name="in_proj_latent")

    (q_a,) = _mm(cqn, w_uq_p, functools.partial(_epi_mla_q, scale=mla_scale),
                 [(w_uq_p.shape[1], w_uq_p.shape[1], BF16)],
                 extras=[(ca_s, (tm, LANES), tab_map), (sa_s, (tm, LANES), tab_map)],
                 tm=tm, tn=w_uq_p.shape[1], chunk=2 * MLA_QK_PAD, resident_w=True, name="mla_q_up")
    (kv_a,) = _mm(ckvn, w_ukv_p, _epi_cast, [(w_ukv_p.shape[1], w_ukv_p.shape[1], BF16)],
                  tm=tm, tn=w_ukv_p.shape[1], chunk=1024, resident_w=True, name="mla_kv_up")
    o_a = _attention(q_a, MLA_QK_PAD,
                     [(kv_a, True, lambda hs: hs), (kpe, False, lambda hs: 0)],
                     kv_a, (True, lambda hs: MLA_HEADS // MLA_HEADS_PER_STEP + hs),
                     n_batch=n_batch, n_tok=n_tok, n_heads=MLA_HEADS, dv=MLA_V, hps=MLA_HEADS_PER_STEP,
                     row_chunk=MLA_ROW_CHUNK, name="mla_attention")

    o_b = _fnet(uv, 0, fnet_tabs, n_batch=n_batch, n_tok=n_tok, gd=gd)

    group = GQA_HEADS // GQA_KV_HEADS
    hps = GQA_HEADS_PER_STEP
    assert group % hps == 0
    o_c = _attention(qk, GQA_HEAD_DIM,
                     [(qk, False, lambda hs: GQA_HEADS + (hs * hps) // group)],
                     uv, (False, lambda hs: fw // GQA_HEAD_DIM + (hs * hps) // group),
                     n_batch=n_batch, n_tok=n_tok, n_heads=GQA_HEADS, dv=GQA_HEAD_DIM, hps=hps,
                     row_chunk=GQA_ROW_CHUNK, name="gqa_attention")

    merged = _merge(o_a, o_b, o_c, wb_bf, gates, d_model)
    otn = _tile(d_model, 512)
    if joined:
        nb_a = x[0].shape[0] // tm
        res_extras = [(x[0], (tm, otn), lambda i, j: (jnp.minimum(i, nb_a - 1), j)),
                      (x[1], (tm, otn), lambda i, j: (jnp.maximum(i - nb_a, 0), j))]
    else:
        nb_a = None
        res_extras = [(x, (tm, otn), lambda i, j: (i, j))]
    ss_out = (LANES, LANES, F32, "row")
    x, h2, ss2 = _mm(merged, wo_bf, functools.partial(_epi_residual_emit, nb_a=nb_a),
                     [(d_model, otn, F32), (d_model, otn, BF16), ss_out],
                     extras=res_extras + [(g_mlp.reshape(1, -1), (1, otn), lambda i, j: (0, j))],
                     tm=tm, tn=512, name="out_proj")

    d_ff = wup_bf.shape[1]
    act, *next_weights = _mm(h2, wup_bf, _epi_relu2, [(d_ff, _tile(d_ff, 1024), BF16)],
                             tm=tm, tn=1024, casts=up_casts, row_ss=ss2, name="mlp_up")
    dtn = _tile(d_model, 1024)
    res = (x, (tm, dtn), lambda i, j: (i, j))
    if g_next is None:
        (x,) = _mm(act, wdown_bf, _epi_residual, [(d_model, dtn, F32)], extras=[res],
                   tm=tm, tn=1024, tk=4096, name="mlp_down")
        return x, next_weights, None
    x, h_next, ss_next = _mm(act, wdown_bf, _epi_residual_emit,
                             [(d_model, dtn, F32), (d_model, dtn, BF16), ss_out],
                             extras=[res, (g_next.reshape(1, -1), (1, dtn), lambda i, j: (0, j))],
                             tm=tm, tn=1024, tk=2048, name="mlp_down")
    return x, next_weights, (h_next, ss_next)


def kernel(x_prompt, x_sample, w_in, g_attn, g_qa, w_uq, g_kva, w_ukv, g_qk_q, g_qk_k,
           w_branch, w_o, g_mlp, w_up, w_down, g_final):
    n_tok, d_model = x_prompt.shape[1], x_prompt.shape[2]
    assert x_sample.shape[1:] == (n_tok, d_model)
    b_p, b_s = x_prompt.shape[0], x_sample.shape[0]
    n_batch = b_p + b_s
    x = (x_prompt.reshape(b_p * n_tok, d_model), x_sample.reshape(b_s * n_tok, d_model))

    mla_scale = (MLA_NOPE + MLA_ROPE) ** -0.5 * LOG2E
    heads_per_chunk = MXU_COLS // GQA_HEAD_DIM
    ones_bd = jnp.kron(jnp.eye(heads_per_chunk, dtype=F32), jnp.ones((GQA_HEAD_DIM, GQA_HEAD_DIM), F32))
    ones_bd = jnp.tile(ones_bd, (2, 1)).astype(BF16)
    tabs = (_rope_tables(n_tok, MLA_ROPE, 1.0), _rope_tables(n_tok, MLA_ROPE, mla_scale),
            _rope_tables(n_tok, GQA_HEAD_DIM, 1.0), ones_bd)
    n_layers, _, fw, _ = w_branch.shape
    fnet_tabs = _fnet_tables(n_tok, fw // FNET_GROUPS)

    wb_rows = w_branch.reshape(n_layers, N_BRANCH * fw, d_model)
    layer_casts = lambda l: [(wb_rows, l), (w_o, l), (w_up, l), (w_down, l)]
    own, gate_casts, h_in = None, layer_casts(0), None
    for l in range(n_layers):
        last = l + 1 == n_layers
        up_casts = [] if last else layer_casts(l + 1)
        x, nxt, h_in = _layer(x, h_in, n_batch, n_tok, tabs, fnet_tabs, w_in[l], g_attn[l], g_qa[l], w_uq[l],
                              g_kva[l], w_ukv[l], g_qk_q[l], g_qk_k[l], g_mlp[l],
                              None if last else g_attn[l + 1], fw, own, gate_casts, up_casts)
        if nxt:
            own, gate_casts = (nxt[0].reshape(N_BRANCH, fw, d_model), *nxt[1:]), []
    y_p, y_s = _rmsnorm_split(x, g_final, b_p * n_tok, F32, "rmsnorm_final")
    return (y_p.reshape(b_p, n_tok, d_model), y_s.reshape(b_s, n_tok, d_model))
```

```python
import functools
import math

import jax
import jax.numpy as jnp
from jax import lax
from jax.experimental import pallas as pl
from jax.experimental.pallas import tpu as pltpu

F32 = jnp.float32
BF16 = jnp.bfloat16

GRID_W = 64
ROPE_THETA = 10000.0
EPS = 1e-6
MLA_HEADS = 16
MLA_NOPE = 128
MLA_ROPE = 64
MLA_V = 128
FNET_GROUPS = 4
GQA_HEADS = 16
GQA_KV_HEADS = 4
GQA_HEAD_DIM = 128
N_BRANCH = 3

LANES = 128
BF16_SUBLANES = 16
MXU_COLS = 256
MLA_QK_PAD = 2 * LANES
VMEM_LIMIT_BYTES = 60 * 2 ** 20
LOG2E = math.log2(math.e)
ATTN_Q_BLOCK = 2048
MLA_ROW_CHUNK = 256
GQA_ROW_CHUNK = 128
MLA_HEADS_PER_STEP = 4
GQA_HEADS_PER_STEP = 4
MERGE_ROWS = 1024
MERGE_COLS = 512


def _tile(n, pref):
    if n <= pref:
        return n
    t = (pref // LANES) * LANES
    while t > LANES and n % t:
        t -= LANES
    assert n % t == 0, (n, pref)
    return t


def _cast_rows(rows, steps):
    br = BF16_SUBLANES
    while rows % br or rows // br > steps:
        br += BF16_SUBLANES
    return br


def _params(*sem):
    return pltpu.CompilerParams(dimension_semantics=sem, vmem_limit_bytes=VMEM_LIMIT_BYTES)


def _rms(x, g):
    return (x * lax.rsqrt(jnp.mean(x * x, axis=-1, keepdims=True) + EPS)) * g


def _rmsnorm_kernel(x_ref, g_ref, o_ref):
    o_ref[...] = _rms(x_ref[...], g_ref[...]).astype(o_ref.dtype)


def _rmsnorm_join_kernel(xa_ref, xb_ref, g_ref, o_ref, *, nb_a):
    i = pl.program_id(0)

    @pl.when(i < nb_a)
    def _():
        o_ref[...] = _rms(xa_ref[...], g_ref[...]).astype(o_ref.dtype)

    @pl.when(i >= nb_a)
    def _():
        o_ref[...] = _rms(xb_ref[...], g_ref[...]).astype(o_ref.dtype)


def _rmsnorm_split_kernel(x_ref, g_ref, oa_ref, ob_ref, *, nb_a):
    i = pl.program_id(0)
    y = _rms(x_ref[...], g_ref[...]).astype(oa_ref.dtype)

    @pl.when(i < nb_a)
    def _():
        oa_ref[...] = y

    @pl.when(i >= nb_a)
    def _():
        ob_ref[...] = y


def _rmsnorm(x, g, out_dtype, name):
    t, d = x.shape
    tm = _tile(t, 256)
    return pl.pallas_call(
        _rmsnorm_kernel,
        grid=(t // tm,),
        in_specs=[pl.BlockSpec((tm, d), lambda i: (i, 0)),
                  pl.BlockSpec((1, d), lambda i: (0, 0))],
        out_specs=pl.BlockSpec((tm, d), lambda i: (i, 0)),
        out_shape=jax.ShapeDtypeStruct((t, d), out_dtype),
        compiler_params=_params("parallel"),
        name=name,
    )(x, g.reshape(1, d))


def _rmsnorm_join(xa, xb, g, out_dtype, name):
    (ta, d), tb = xa.shape, xb.shape[0]
    tm = _tile(math.gcd(ta, tb), 256)
    nb_a = ta // tm
    return pl.pallas_call(
        functools.partial(_rmsnorm_join_kernel, nb_a=nb_a),
        grid=((ta + tb) // tm,),
        in_specs=[pl.BlockSpec((tm, d), lambda i: (jnp.minimum(i, nb_a - 1), 0)),
                  pl.BlockSpec((tm, d), lambda i: (jnp.maximum(i - nb_a, 0), 0)),
                  pl.BlockSpec((1, d), lambda i: (0, 0))],
        out_specs=pl.BlockSpec((tm, d), lambda i: (i, 0)),
        out_shape=jax.ShapeDtypeStruct((ta + tb, d), out_dtype),
        compiler_params=_params("arbitrary"),
        name=name,
    )(xa, xb, g.reshape(1, d))


def _rmsnorm_split(x, g, ta, out_dtype, name):
    t, d = x.shape
    tm = _tile(math.gcd(ta, t - ta), 256)
    nb_a = ta // tm
    return pl.pallas_call(
        functools.partial(_rmsnorm_split_kernel, nb_a=nb_a),
        grid=(t // tm,),
        in_specs=[pl.BlockSpec((tm, d), lambda i: (i, 0)),
                  pl.BlockSpec((1, d), lambda i: (0, 0))],
        out_specs=[pl.BlockSpec((tm, d), lambda i: (jnp.minimum(i, nb_a - 1), 0)),
                   pl.BlockSpec((tm, d), lambda i: (jnp.maximum(i - nb_a, 0), 0))],
        out_shape=[jax.ShapeDtypeStruct((ta, d), out_dtype),
                   jax.ShapeDtypeStruct((t - ta, d), out_dtype)],
        compiler_params=_params("arbitrary"),
        name=name,
    )(x, g.reshape(1, d))


def _mm_kernel(a_ref, w_ref, *refs, epi, n_extra, n_cast, n_out, nk, chunks):
    extras = refs[:n_extra]
    cast_in = refs[n_extra:n_extra + n_cast]
    outs = refs[n_extra + n_cast:n_extra + n_cast + n_out]
    cast_out = refs[n_extra + n_cast + n_out:]
    for ci, co in zip(cast_in, cast_out):
        co[...] = ci[...].astype(co.dtype)
    if nk == 1:
        for c0, cw in chunks:
            acc = jnp.dot(a_ref[...], w_ref[:, c0:c0 + cw], preferred_element_type=F32)
            epi(acc, c0, extras, outs)
        return
    assert epi is _epi_residual
    k = pl.program_id(2)
    d = jnp.dot(a_ref[...], w_ref[...], preferred_element_type=F32)

    @pl.when(k == 0)
    def _():
        outs[0][...] = extras[0][...] + d

    @pl.when(k > 0)
    def _():
        outs[0][...] += d


def _mm(a, w, epi, outs, extras=(), *, tm, tn, tk=None, chunk=None, chunks=None, layer=None,
        resident_w=False, casts=(), row_ss=None, name):
    m, kdim = a.shape
    n = w.shape[-1]
    tm = _tile(m, tm)
    tn = _tile(n, tn)
    tk = kdim if tk is None else _tile(kdim, tk)
    nk = kdim // tk
    if chunks is None:
        cw = tn if chunk is None else min(chunk, tn)
        chunks = [(c0, cw) for c0 in range(0, tn, cw)]
    assert nk == 1 or len(chunks) == 1
    grid = (m // tm, n // tn, nk)
    w_mode = dict(pipeline_mode=pl.Buffered(1)) if resident_w else {}
    assert not resident_w or (n == tn and nk == 1)
    if w.ndim == 3:
        w_spec = pl.BlockSpec((None, tk, tn), lambda i, j, k: (layer, k, j), **w_mode)
    else:
        w_spec = pl.BlockSpec((tk, tn), lambda i, j, k: (k, j), **w_mode)
    in_specs = [pl.BlockSpec((tm, tk), lambda i, j, k: (i, k)), w_spec]
    extras = list(extras)
    if row_ss is not None:
        extras.append((row_ss, (tm, LANES), lambda i, j: (i, 0)))
        epi = _row_scaled(epi, kdim)
    for _, bshape, imap in extras:
        in_specs.append(pl.BlockSpec(bshape, lambda i, j, k, imap=imap: imap(i, j)))
    out_specs = [pl.BlockSpec((tm, o[1]), (lambda i, j, k: (i, 0)) if o[3:] == ("row",) else
                              (lambda i, j, k: (i, j))) for o in outs]
    out_shape = [jax.ShapeDtypeStruct((m, o[0]), o[2]) for o in outs]
    nj = n // tn
    for arr, lyr in casts:
        rows, cols = arr.shape[-2:]
        br = _cast_rows(rows, grid[0] * nj * nk)
        nb = rows // br
        blk = lambda i, j, k, nb=nb: jnp.minimum((i * nj + j) * nk + k, nb - 1)
        in_specs.append(pl.BlockSpec((None, br, cols), lambda i, j, k, lyr=lyr, blk=blk: (lyr, blk(i, j, k), 0)))
        out_specs.append(pl.BlockSpec((br, cols), lambda i, j, k, blk=blk: (blk(i, j, k), 0)))
        out_shape.append(jax.ShapeDtypeStruct((rows, cols), BF16))
    revisits = casts or any(o[3:] == ("row",) for o in outs)
    sem = ("arbitrary",) * 3 if revisits else ("parallel", "parallel", "arbitrary")
    return pl.pallas_call(
        functools.partial(_mm_kernel, epi=epi, n_extra=len(extras), n_cast=len(casts), n_out=len(outs),
                          nk=nk, chunks=chunks),
        grid=grid,
        in_specs=in_specs,
        out_specs=out_specs,
        out_shape=out_shape,
        compiler_params=_params(*sem),
        name=name,
    )(a, w, *[e[0] for e in extras], *[c[0] for c in casts])


def _cols(c0, acc):
    return slice(c0, c0 + acc.shape[1])


def _epi_cast(acc, c0, extras, outs):
    outs[0][:, _cols(c0, acc)] = acc.astype(outs[0].dtype)


def _epi_relu2(acc, c0, extras, outs):
    r = jnp.maximum(acc, 0.0)
    outs[0][:, _cols(c0, acc)] = (r * r).astype(outs[0].dtype)


def _epi_residual(acc, c0, extras, outs):
    cs = _cols(c0, acc)
    outs[0][:, cs] = extras[0][:, cs] + acc


def _epi_residual_join(acc, c0, extras, outs, *, nb_a):
    cs = _cols(c0, acc)
    i = pl.program_id(0)

    @pl.when(i < nb_a)
    def _():
        outs[0][:, cs] = extras[0][:, cs] + acc

    @pl.when(i >= nb_a)
    def _():
        outs[0][:, cs] = extras[1][:, cs] + acc


def _row_sumsq(x):
    return jnp.broadcast_to(jnp.sum(x * x, axis=-1, keepdims=True), (x.shape[0], LANES))


def _epi_residual_emit(acc, c0, extras, outs, *, nb_a=None):
    assert c0 == 0 and acc.shape[1] == outs[0].shape[1]
    if nb_a is None:
        _epi_residual(acc, c0, extras, outs)
    else:
        _epi_residual_join(acc, c0, extras, outs, nb_a=nb_a)
    x = outs[0][...]
    outs[1][...] = (x * extras[-1][...]).astype(outs[1].dtype)
    part = _row_sumsq(x)
    j = pl.program_id(1)

    @pl.when(j == 0)
    def _():
        outs[2][...] = part

    @pl.when(j > 0)
    def _():
        outs[2][...] += part


def _row_scaled(epi, d):
    def wrapped(acc, c0, extras, outs):
        r = lax.rsqrt(extras[-1][...] * (1.0 / d) + EPS)
        epi(acc * jnp.tile(r, (1, acc.shape[1] // LANES)), c0, extras[:-1], outs)
    return wrapped


def _rope(x, c, s):
    return x * c + pltpu.roll(x, LANES // 2, 1) * s


def _epi_latent(acc, c0, extras, outs, *, q_lora, kv_lora):
    gq, gkv, c, s = extras
    if c0 == 0:
        outs[0][...] = _rms(acc, gq[...]).astype(BF16)
    elif c0 == q_lora:
        outs[1][...] = _rms(acc, gkv[...]).astype(BF16)
    else:
        outs[2][...] = _rope(acc, c[...], s[...]).astype(BF16)


def _epi_gqa_heads(acc, c0, extras, outs):
    g, c, s, ones_bd = extras
    cv, sv = c[...], s[...]
    for g0 in range(0, acc.shape[1], MXU_COLS):
        x = acc[:, g0:g0 + MXU_COLS]
        x2 = x * x
        hi = x2.astype(BF16)
        lo = (x2 - hi.astype(F32)).astype(BF16)
        ss = jnp.dot(jnp.concatenate([hi, lo], axis=1), ones_bd[...], preferred_element_type=F32)
        y = (x * lax.rsqrt(ss * (1.0 / GQA_HEAD_DIM) + EPS)) * g[:, c0 + g0:c0 + g0 + MXU_COLS]
        for h0 in range(0, MXU_COLS, GQA_HEAD_DIM):
            osl = slice(c0 + g0 + h0, c0 + g0 + h0 + GQA_HEAD_DIM)
            outs[0][:, osl] = _rope(y[:, h0:h0 + GQA_HEAD_DIM], cv, sv).astype(BF16)


def _epi_mla_q(acc, c0, extras, outs, *, scale):
    c, s = extras
    cv, sv = c[...], s[...]
    for h in range(acc.shape[1] // MLA_QK_PAD):
        lo = slice(h * MLA_QK_PAD, h * MLA_QK_PAD + LANES)
        hi = slice(h * MLA_QK_PAD + LANES, (h + 1) * MLA_QK_PAD)
        olo = slice(c0 + lo.start, c0 + lo.stop)
        ohi = slice(c0 + hi.start, c0 + hi.stop)
        outs[0][:, olo] = (acc[:, lo] * scale).astype(BF16)
        outs[0][:, ohi] = _rope(acc[:, hi], cv, sv).astype(BF16)


def _attn_kernel(q_ref, *refs, n_k, k_per_head, v_per_head, hps, rows):
    k_refs, v_ref, o_ref = refs[:n_k], refs[n_k], refs[n_k + 1]
    qc = q_ref.shape[1] // hps
    dv = o_ref.shape[1] // hps

    def head_kv(hh):
        parts = [r[:, hh * LANES:(hh + 1) * LANES] if ph else r[...] for r, ph in zip(k_refs, k_per_head)]
        k = parts[0] if len(parts) == 1 else jnp.concatenate(parts, axis=-1)
        v = v_ref[:, hh * dv:(hh + 1) * dv] if v_per_head else v_ref[...]
        return k, jnp.concatenate([v, jnp.ones_like(v)], axis=-1)

    shared = not (any(k_per_head) or v_per_head)
    if shared:
        k, v1 = head_kv(0)
    for hh in range(hps):
        if not shared:
            k, v1 = head_kv(hh)
        for r0 in range(0, q_ref.shape[0], rows):
            q = q_ref[r0:r0 + rows, hh * qc:(hh + 1) * qc]
            s = lax.dot_general(q, k, (((1,), (1,)), ((), ())), preferred_element_type=F32)
            m = jnp.max(s, axis=-1, keepdims=True)
            p = jnp.exp2(s - m).astype(BF16)
            o = jnp.dot(p, v1, preferred_element_type=F32)
            o_ref[r0:r0 + rows, hh * dv:(hh + 1) * dv] = (o[:, :dv] / o[:, dv:]).astype(o_ref.dtype)


def _attention(q, q_cols, ks, v, v_spec, *, n_batch, n_tok, n_heads, dv, hps, row_chunk, name):
    tq = _tile(n_tok, ATTN_Q_BLOCK)
    nq = n_tok // tq
    rows = min(tq, row_chunk)
    assert n_heads % hps == 0
    in_specs = [pl.BlockSpec((tq, hps * q_cols), lambda b, h, i: (b * nq + i, h))]
    args = [q]
    for arr, per_head, colfn in ks:
        cols = hps * LANES if per_head else LANES
        in_specs.append(pl.BlockSpec((n_tok, cols), lambda b, h, i, colfn=colfn: (b, colfn(h))))
        args.append(arr)
    v_per_head, v_colfn = v_spec
    in_specs.append(pl.BlockSpec((n_tok, hps * dv if v_per_head else dv), lambda b, h, i: (b, v_colfn(h))))
    args.append(v)
    return pl.pallas_call(
        functools.partial(_attn_kernel, n_k=len(ks), k_per_head=tuple(k[1] for k in ks),
                          v_per_head=v_per_head, hps=hps, rows=rows),
        grid=(n_batch, n_heads // hps, nq),
        in_specs=in_specs,
        out_specs=pl.BlockSpec((tq, hps * dv), lambda b, h, i: (b * nq + i, h)),
        out_shape=jax.ShapeDtypeStruct((n_batch * n_tok, n_heads * dv), BF16),
        compiler_params=_params("parallel", "parallel", "parallel"),
        name=name,
    )(*args)


def _fnet_chan_kernel(u_ref, w_ref, yc_ref, ys_ref):
    gd = yc_ref.shape[1]
    y = jnp.dot(u_ref[...], w_ref[...], preferred_element_type=F32)
    yc_ref[...] = y[:, :gd].astype(BF16)
    ys_ref[...] = y[:, gd:].astype(BF16)


def _fnet_pos_kernel(cn_ref, sn_ref, yc_ref, ys_ref, o_ref):
    o = jnp.dot(cn_ref[...], yc_ref[...], preferred_element_type=F32)
    o = o + jnp.dot(sn_ref[...], ys_ref[...], preferred_element_type=F32)
    o_ref[...] = o.astype(o_ref.dtype)


def _dft_tables(n):
    idx = jnp.arange(n, dtype=jnp.int32)
    jk = (idx[:, None] * idx[None, :]) % n
    ang = jk.astype(F32) * (2.0 * math.pi / n)
    return jnp.cos(ang), jnp.sin(ang)


def _fnet_tables(n_tok, gd):
    norm = 1.0 / math.sqrt(n_tok * gd)
    s_chan = 2.0 ** round(math.log2(norm) / 2)
    s_pos = norm / s_chan
    cc, sc = _dft_tables(gd)
    w_chan = (jnp.concatenate([cc, sc], axis=1) * s_chan).astype(BF16)
    cn, sn = _dft_tables(n_tok)
    return w_chan, (cn * s_pos).astype(BF16), (sn * (-s_pos)).astype(BF16)


def _fnet(z, u_col0, tables, *, n_batch, n_tok, gd):
    t = z.shape[0]
    width = FNET_GROUPS * gd
    w_chan, cn, msn = tables
    tm = _tile(t, 2048)
    ublk = u_col0 // gd
    assert u_col0 % gd == 0
    yc, ys = pl.pallas_call(
        _fnet_chan_kernel,
        grid=(t // tm, FNET_GROUPS),
        in_specs=[pl.BlockSpec((tm, gd), lambda i, g: (i, ublk + g)),
                  pl.BlockSpec((gd, 2 * gd), lambda i, g: (0, 0))],
        out_specs=[pl.BlockSpec((tm, gd), lambda i, g: (i, g))] * 2,
        out_shape=[jax.ShapeDtypeStruct((t, width), BF16)] * 2,
        compiler_params=_params("parallel", "parallel"),
        name="fnet_chan",
    )(z, w_chan)

    tp = _tile(n_tok, 1024)
    tn = _tile(width, 1024)
    npb = n_tok // tp
    return pl.pallas_call(
        _fnet_pos_kernel,
        grid=(n_batch, width // tn, npb),
        in_specs=[pl.BlockSpec((tp, n_tok), lambda b, j, i: (i, 0)),
                  pl.BlockSpec((tp, n_tok), lambda b, j, i: (i, 0)),
                  pl.BlockSpec((n_tok, tn), lambda b, j, i: (b, j)),
                  pl.BlockSpec((n_tok, tn), lambda b, j, i: (b, j))],
        out_specs=pl.BlockSpec((tp, tn), lambda b, j, i: (b * npb + i, j)),
        out_shape=jax.ShapeDtypeStruct((t, width), BF16),
        compiler_params=_params("parallel", "parallel", "parallel"),
        name="fnet_pos",
    )(cn, msn, yc, ys)


def _merge_kernel(oa_ref, ob_ref, oc_ref, w_ref, ga_ref, gb_ref, gc_ref, out_ref):
    tn = out_ref.shape[1]
    cw = min(tn, MXU_COLS)
    for c0 in range(0, tn, cw):
        cs = slice(c0, c0 + cw)
        acc = None
        for b, (o_ref, g_ref) in enumerate(((oa_ref, ga_ref), (ob_ref, gb_ref), (oc_ref, gc_ref))):
            gate = 1.0 / (1.0 + jnp.exp(-g_ref[:, cs].astype(F32)))
            c = gate * jnp.dot(o_ref[...], w_ref[b, :, cs], preferred_element_type=F32)
            acc = c if acc is None else acc + c
        out_ref[:, cs] = acc.astype(out_ref.dtype)


def _merge(oa, ob, oc, w_branch, gates, d_model):
    t, bw = oa.shape
    tm = _tile(t, MERGE_ROWS)
    tn = _tile(d_model, MERGE_COLS)
    nj = d_model // tn
    o_spec = pl.BlockSpec((tm, bw), lambda i, j: (i, 0))
    g_specs = [pl.BlockSpec((tm, tn), lambda i, j, b=b: (i, b * nj + j)) for b in range(N_BRANCH)]
    return pl.pallas_call(
        _merge_kernel,
        grid=(t // tm, nj),
        in_specs=[o_spec, o_spec, o_spec,
                  pl.BlockSpec((N_BRANCH, bw, tn), lambda i, j: (0, 0, j))] + g_specs,
        out_specs=pl.BlockSpec((tm, tn), lambda i, j: (i, j)),
        out_shape=jax.ShapeDtypeStruct((t, d_model), BF16),
        compiler_params=_params("parallel", "parallel"),
        name="gated_merge",
    )(oa, ob, oc, w_branch, gates, gates, gates)


def _rope_tables(n_tok, rot_dim, scale):
    rows = n_tok // GRID_W
    row_idx = jnp.broadcast_to(jnp.arange(rows)[:, None], (rows, GRID_W)).reshape(-1).astype(F32)
    col_idx = jnp.broadcast_to(jnp.arange(GRID_W)[None, :], (rows, GRID_W)).reshape(-1).astype(F32)
    nq = rot_dim // 4
    freqs = ROPE_THETA ** (-(2.0 * jnp.arange(nq, dtype=F32)) / (rot_dim // 2))
    ang = jnp.stack([row_idx[:, None] * freqs, col_idx[:, None] * freqs], axis=1)
    cos, sin = jnp.cos(ang), jnp.sin(ang)
    pad = jnp.zeros((n_tok, LANES // 2 - 2 * nq), F32)
    cblk = jnp.concatenate([cos[:, 0], cos[:, 1], pad], axis=1)
    sblk = jnp.concatenate([sin[:, 0], sin[:, 1], pad], axis=1)
    c = jnp.concatenate([cblk, cblk], axis=1)
    s = jnp.concatenate([-sblk, sblk], axis=1)
    return c * scale, s * scale


def _rotary_lanes(w, rot_dim):
    nq = rot_dim // 4
    lead = w.shape[:-1]
    r = jnp.swapaxes(w.reshape(*lead, 2, 2, nq), -3, -2).reshape(*lead, 2, 2 * nq)
    r = jnp.pad(r, [(0, 0)] * (len(lead) + 1) + [(0, LANES // 2 - 2 * nq)])
    return r.reshape(*lead, LANES)


def _layer_weights(w_in, w_uq, w_ukv, g_qk_q, g_qk_k, q_lora, kv_lora, d_model, fw, gqa_scale):
    seg = [q_lora, kv_lora, MLA_ROPE, fw, GQA_HEADS * GQA_HEAD_DIM,
           GQA_KV_HEADS * GQA_HEAD_DIM, GQA_KV_HEADS * GQA_HEAD_DIM, N_BRANCH * d_model]
    offs = [0]
    for s in seg:
        offs.append(offs[-1] + s)
    cq, ckv, kpe, uf, qc, kc, vc, gl = [w_in[:, offs[i]:offs[i + 1]] for i in range(8)]
    w_gate = gl.astype(BF16)
    w_uv = jnp.concatenate([uf, vc], axis=1).astype(BF16)
    d_in = w_in.shape[0]
    qc = _rotary_lanes(qc.reshape(d_in, GQA_HEADS, GQA_HEAD_DIM), GQA_HEAD_DIM).reshape(d_in, -1)
    kc = _rotary_lanes(kc.reshape(d_in, GQA_KV_HEADS, GQA_HEAD_DIM), GQA_HEAD_DIM).reshape(d_in, -1)
    w_qk = jnp.concatenate([qc, kc], axis=1).astype(BF16)
    w_lat = jnp.concatenate([cq, ckv, _rotary_lanes(kpe, MLA_ROPE)], axis=1).astype(BF16)
    g_qk = jnp.concatenate([jnp.tile(_rotary_lanes(g_qk_q, GQA_HEAD_DIM) * gqa_scale, GQA_HEADS),
                            jnp.tile(_rotary_lanes(g_qk_k, GQA_HEAD_DIM), GQA_KV_HEADS)]).reshape(1, -1)
    uq = w_uq.reshape(q_lora, MLA_HEADS, MLA_NOPE + MLA_ROPE)
    uq = jnp.concatenate([uq[:, :, :MLA_NOPE], _rotary_lanes(uq[:, :, MLA_NOPE:], MLA_ROPE)], axis=2)
    w_uq_p = uq.reshape(q_lora, MLA_HEADS * MLA_QK_PAD).astype(BF16)
    ukv = w_ukv.reshape(kv_lora, MLA_HEADS, MLA_NOPE + MLA_V)
    w_ukv_p = jnp.concatenate([ukv[:, :, :MLA_NOPE].reshape(kv_lora, -1),
                               ukv[:, :, MLA_NOPE:].reshape(kv_lora, -1)], axis=1).astype(BF16)
    return w_gate, w_uv, w_qk, w_lat, g_qk, w_uq_p, w_ukv_p


def _layer(x, n_batch, n_tok, tabs, fnet_tabs, w_in, g_attn, g_qa, w_uq, g_kva, w_ukv,
           g_qk_q, g_qk_k, g_mlp, fw, own, gate_casts, up_casts):
    joined = isinstance(x, tuple)
    d_model = g_attn.shape[0]
    q_lora, kv_lora = g_qa.shape[0], g_kva.shape[0]
    gd = fw // FNET_GROUPS
    mla_scale = (MLA_NOPE + MLA_ROPE) ** -0.5 * LOG2E
    gqa_scale = GQA_HEAD_DIM ** -0.5 * LOG2E
    (ca, sa), (ca_s, sa_s), (cc, sc), ones_bd = tabs
    w_gate, w_uv, w_qk, w_lat, g_qk, w_uq_p, w_ukv_p = _layer_weights(
        w_in, w_uq, w_ukv, g_qk_q, g_qk_k, q_lora, kv_lora, d_model, fw, gqa_scale)

    tm = _tile(n_tok, 1024)
    nrb = n_tok // tm
    tab_map = lambda i, j: (i % nrb, 0)

    if joined:
        h = _rmsnorm_join(x[0], x[1], g_attn, BF16, "rmsnorm_attn")
    else:
        h = _rmsnorm(x, g_attn, BF16, "rmsnorm_attn")

    gates, *cast_out = _mm(h, w_gate, _epi_cast, [(w_gate.shape[1], _tile(w_gate.shape[1], 1024), BF16)],
                           tm=tm, tn=1024, casts=gate_casts, name="in_proj_gates")
    if own is None:
        wb2d, wo_bf, wup_bf, wdown_bf = cast_out
        wb_bf = wb2d.reshape(N_BRANCH, fw, d_model)
    else:
        wb_bf, wo_bf, wup_bf, wdown_bf = own
    (uv,) = _mm(h, w_uv, _epi_cast, [(w_uv.shape[1], _tile(w_uv.shape[1], 1280), BF16)],
                tm=tm, tn=1280, name="in_proj_uv")
    qk_tn = _tile(w_qk.shape[1], 512)
    (qk,) = _mm(h, w_qk, _epi_gqa_heads, [(w_qk.shape[1], qk_tn, BF16)],
                extras=[(g_qk, (1, qk_tn), lambda i, j: (0, j)),
                        (cc, (tm, LANES), tab_map), (sc, (tm, LANES), tab_map),
                        (ones_bd, ones_bd.shape, lambda i, j: (0, 0))],
                tm=tm, tn=512, name="in_proj_gqa_qk")
    cqn, ckvn, kpe = _mm(
        h, w_lat, functools.partial(_epi_latent, q_lora=q_lora, kv_lora=kv_lora),
        [(q_lora, q_lora, BF16), (kv_lora, kv_lora, BF16), (LANES, LANES, BF16)],
        extras=[(g_qa.reshape(1, -1), (1, q_lora), lambda i, j: (0, 0)),
                (g_kva.reshape(1, -1), (1, kv_lora), lambda i, j: (0, 0)),
                (ca, (tm, LANES), tab_map), (sa, (tm, LANES), tab_map)],
        tm=tm, tn=w_lat.shape[1], resident_w=True,
        chunks=[(0, q_lora), (q_lora, kv_lora), (q_lora + kv_lora, LANES)], name="in_proj_latent")

    (q_a,) = _mm(cqn, w_uq_p, functools.partial(_epi_mla_q, scale=mla_scale),
                 [(w_uq_p.shape[1], w_uq_p.shape[1], BF16)],
                 extras=[(ca_s, (tm, LANES), tab_map), (sa_s, (tm, LANES), tab_map)],
                 tm=tm, tn=w_uq_p.shape[1], chunk=2 * MLA_QK_PAD, resident_w=True, name="mla_q_up")
    (kv_a,) = _mm(ckvn, w_ukv_p, _epi_cast, [(w_ukv_p.shape[1], w_ukv_p.shape[1], BF16)],
                  tm=tm, tn=w_ukv_p.shape[1], chunk=1024, resident_w=True, name="mla_kv_up")
    o_a = _attention(q_a, MLA_QK_PAD,
                     [(kv_a, True, lambda hs: hs), (kpe, False, lambda hs: 0)],
                     kv_a, (True, lambda hs: MLA_HEADS // MLA_HEADS_PER_STEP + hs),
                     n_batch=n_batch, n_tok=n_tok, n_heads=MLA_HEADS, dv=MLA_V, hps=MLA_HEADS_PER_STEP,
                     row_chunk=MLA_ROW_CHUNK, name="mla_attention")

    o_b = _fnet(uv, 0, fnet_tabs, n_batch=n_batch, n_tok=n_tok, gd=gd)

    group = GQA_HEADS // GQA_KV_HEADS
    hps = GQA_HEADS_PER_STEP
    assert group % hps == 0
    o_c = _attention(qk, GQA_HEAD_DIM,
                     [(qk, False, lambda hs: GQA_HEADS + (hs * hps) // group)],
                     uv, (False, lambda hs: fw // GQA_HEAD_DIM + (hs * hps) // group),
                     n_batch=n_batch, n_tok=n_tok, n_heads=GQA_HEADS, dv=GQA_HEAD_DIM, hps=hps,
                     row_chunk=GQA_ROW_CHUNK, name="gqa_attention")

    merged = _merge(o_a, o_b, o_c, wb_bf, gates, d_model)
    otn = _tile(d_model, 512)
    if joined:
        nb_a = x[0].shape[0] // tm
        res_extras = [(x[0], (tm, otn), lambda i, j: (jnp.minimum(i, nb_a - 1), j)),
                      (x[1], (tm, otn), lambda i, j: (jnp.maximum(i - nb_a, 0), j))]
    else:
        nb_a = None
        res_extras = [(x, (tm, otn), lambda i, j: (i, j))]
    ss_out = (LANES, LANES, F32, "row")
    x, h2, ss2 = _mm(merged, wo_bf, functools.partial(_epi_residual_emit, nb_a=nb_a),
                     [(d_model, otn, F32), (d_model, otn, BF16), ss_out],
                     extras=res_extras + [(g_mlp.reshape(1, -1), (1, otn), lambda i, j: (0, j))],
                     tm=tm, tn=512, name="out_proj")

    d_ff = wup_bf.shape[1]
    act, *next_weights = _mm(h2, wup_bf, _epi_relu2, [(d_ff, _tile(d_ff, 1024), BF16)],
                             tm=tm, tn=1024, casts=up_casts, row_ss=ss2, name="mlp_up")
    dtn = _tile(d_model, 1024)
    (x,) = _mm(act, wdown_bf, _epi_residual, [(d_model, dtn, F32)],
               extras=[(x, (tm, dtn), lambda i, j: (i, j))],
               tm=tm, tn=1024, tk=4096, name="mlp_down")
    return x, next_weights


def kernel(x_prompt, x_sample, w_in, g_attn, g_qa, w_uq, g_kva, w_ukv, g_qk_q, g_qk_k,
           w_branch, w_o, g_mlp, w_up, w_down, g_final):
    n_tok, d_model = x_prompt.shape[1], x_prompt.shape[2]
    assert x_sample.shape[1:] == (n_tok, d_model)
    b_p, b_s = x_prompt.shape[0], x_sample.shape[0]
    n_batch = b_p + b_s
    x = (x_prompt.reshape(b_p * n_tok, d_model), x_sample.reshape(b_s * n_tok, d_model))

    mla_scale = (MLA_NOPE + MLA_ROPE) ** -0.5 * LOG2E
    heads_per_chunk = MXU_COLS // GQA_HEAD_DIM
    ones_bd = jnp.kron(jnp.eye(heads_per_chunk, dtype=F32), jnp.ones((GQA_HEAD_DIM, GQA_HEAD_DIM), F32))
    ones_bd = jnp.tile(ones_bd, (2, 1)).astype(BF16)
    tabs = (_rope_tables(n_tok, MLA_ROPE, 1.0), _rope_tables(n_tok, MLA_ROPE, mla_scale),
            _rope_tables(n_tok, GQA_HEAD_DIM, 1.0), ones_bd)
    n_layers, _, fw, _ = w_branch.shape
    fnet_tabs = _fnet_tables(n_tok, fw // FNET_GROUPS)

    wb_rows = w_branch.reshape(n_layers, N_BRANCH * fw, d_model)
    layer_casts = lambda l: [(wb_rows, l), (w_o, l), (w_up, l), (w_down, l)]
    own, gate_casts = None, layer_casts(0)
    for l in range(n_layers):
        up_casts = layer_casts(l + 1) if l + 1 < n_layers else []
        x, nxt = _layer(x, n_batch, n_tok, tabs, fnet_tabs, w_in[l], g_attn[l], g_qa[l], w_uq[l], g_kva[l],
                        w_ukv[l], g_qk_q[l], g_qk_k[l], g_mlp[l], fw, own, gate_casts, up_casts)
        if nxt:
            own, gate_casts = (nxt[0].reshape(N_BRANCH, fw, d_model), *nxt[1:]), []
    y_p, y_s = _rmsnorm_split(x, g_final, b_p * n_tok, F32, "rmsnorm_final")
    return (y_p.reshape(b_p, n_tok, d_model), y_s.reshape(b_s, n_tok, d_model))
```

```python
import functools
import math

import jax
import jax.numpy as jnp
from jax import lax
from jax.experimental import pallas as pl
from jax.experimental.pallas import tpu as pltpu

F32 = jnp.float32
BF16 = jnp.bfloat16

GRID_W = 64
ROPE_THETA = 10000.0
EPS = 1e-6
MLA_HEADS = 16
MLA_NOPE = 128
MLA_ROPE = 64
MLA_V = 128
FNET_GROUPS = 4
GQA_HEADS = 16
GQA_KV_HEADS = 4
GQA_HEAD_DIM = 128
N_BRANCH = 3

LANES = 128
BF16_SUBLANES = 16
MXU_COLS = 256
MLA_QK_PAD = 2 * LANES
VMEM_LIMIT_BYTES = 60 * 2 ** 20
LOG2E = math.log2(math.e)
ATTN_Q_BLOCK = 2048
MLA_ROW_CHUNK = 256
GQA_ROW_CHUNK = 128
MLA_HEADS_PER_STEP = 4
GQA_HEADS_PER_STEP = 4
MERGE_ROWS = 1024
MERGE_COLS = 512


def _tile(n, pref):
    if n <= pref:
        return n
    t = (pref // LANES) * LANES
    while t > LANES and n % t:
        t -= LANES
    assert n % t == 0, (n, pref)
    return t


def _cast_rows(rows, steps):
    br = BF16_SUBLANES
    while rows % br or rows // br > steps:
        br += BF16_SUBLANES
    return br


def _params(*sem):
    return pltpu.CompilerParams(dimension_semantics=sem, vmem_limit_bytes=VMEM_LIMIT_BYTES)


def _rms(x, g):
    return (x * lax.rsqrt(jnp.mean(x * x, axis=-1, keepdims=True) + EPS)) * g


def _rmsnorm_kernel(x_ref, g_ref, o_ref):
    o_ref[...] = _rms(x_ref[...], g_ref[...]).astype(o_ref.dtype)


def _rmsnorm_join_kernel(xa_ref, xb_ref, g_ref, o_ref, *, nb_a):
    i = pl.program_id(0)

    @pl.when(i < nb_a)
    def _():
        o_ref[...] = _rms(xa_ref[...], g_ref[...]).astype(o_ref.dtype)

    @pl.when(i >= nb_a)
    def _():
        o_ref[...] = _rms(xb_ref[...], g_ref[...]).astype(o_ref.dtype)


def _rmsnorm_split_kernel(x_ref, g_ref, oa_ref, ob_ref, *, nb_a):
    i = pl.program_id(0)
    y = _rms(x_ref[...], g_ref[...]).astype(oa_ref.dtype)

    @pl.when(i < nb_a)
    def _():
        oa_ref[...] = y

    @pl.when(i >= nb_a)
    def _():
        ob_ref[...] = y


def _rmsnorm(x, g, out_dtype, name):
    t, d = x.shape
    tm = _tile(t, 256)
    return pl.pallas_call(
        _rmsnorm_kernel,
        grid=(t // tm,),
        in_specs=[pl.BlockSpec((tm, d), lambda i: (i, 0)),
                  pl.BlockSpec((1, d), lambda i: (0, 0))],
        out_specs=pl.BlockSpec((tm, d), lambda i: (i, 0)),
        out_shape=jax.ShapeDtypeStruct((t, d), out_dtype),
        compiler_params=_params("parallel"),
        name=name,
    )(x, g.reshape(1, d))


def _rmsnorm_join(xa, xb, g, out_dtype, name):
    (ta, d), tb = xa.shape, xb.shape[0]
    tm = _tile(math.gcd(ta, tb), 256)
    nb_a = ta // tm
    return pl.pallas_call(
        functools.partial(_rmsnorm_join_kernel, nb_a=nb_a),
        grid=((ta + tb) // tm,),
        in_specs=[pl.BlockSpec((tm, d), lambda i: (jnp.minimum(i, nb_a - 1), 0)),
                  pl.BlockSpec((tm, d), lambda i: (jnp.maximum(i - nb_a, 0), 0)),
                  pl.BlockSpec((1, d), lambda i: (0, 0))],
        out_specs=pl.BlockSpec((tm, d), lambda i: (i, 0)),
        out_shape=jax.ShapeDtypeStruct((ta + tb, d), out_dtype),
        compiler_params=_params("arbitrary"),
        name=name,
    )(xa, xb, g.reshape(1, d))


def _rmsnorm_split(x, g, ta, out_dtype, name):
    t, d = x.shape
    tm = _tile(math.gcd(ta, t - ta), 256)
    nb_a = ta // tm
    return pl.pallas_call(
        functools.partial(_rmsnorm_split_kernel, nb_a=nb_a),
        grid=(t // tm,),
        in_specs=[pl.BlockSpec((tm, d), lambda i: (i, 0)),
                  pl.BlockSpec((1, d), lambda i: (0, 0))],
        out_specs=[pl.BlockSpec((tm, d), lambda i: (jnp.minimum(i, nb_a - 1), 0)),
                   pl.BlockSpec((tm, d), lambda i: (jnp.maximum(i - nb_a, 0), 0))],
        out_shape=[jax.ShapeDtypeStruct((ta, d), out_dtype),
                   jax.ShapeDtypeStruct((t - ta, d), out_dtype)],
        compiler_params=_params("arbitrary"),
        name=name,
    )(x, g.reshape(1, d))


def _mm_kernel(a_ref, w_ref, *refs, epi, n_extra, n_cast, n_out, nk, chunks):
    extras = refs[:n_extra]
    cast_in = refs[n_extra:n_extra + n_cast]
    outs = refs[n_extra + n_cast:n_extra + n_cast + n_out]
    cast_out = refs[n_extra + n_cast + n_out:]
    for ci, co in zip(cast_in, cast_out):
        co[...] = ci[...].astype(co.dtype)
    if nk == 1:
        for c0, cw in chunks:
            acc = jnp.dot(a_ref[...], w_ref[:, c0:c0 + cw], preferred_element_type=F32)
            epi(acc, c0, extras, outs)
        return
    assert epi is _epi_residual
    k = pl.program_id(2)
    d = jnp.dot(a_ref[...], w_ref[...], preferred_element_type=F32)

    @pl.when(k == 0)
    def _():
        outs[0][...] = extras[0][...] + d

    @pl.when(k > 0)
    def _():
        outs[0][...] += d


def _mm(a, w, epi, outs, extras=(), *, tm, tn, tk=None, chunk=None, chunks=None, layer=None,
        resident_w=False, casts=(), row_ss=None, name):
    m, kdim = a.shape
    n = w.shape[-1]
    tm = _tile(m, tm)
    tn = _tile(n, tn)
    tk = kdim if tk is None else _tile(kdim, tk)
    nk = kdim // tk
    if chunks is None:
        cw = tn if chunk is None else min(chunk, tn)
        chunks = [(c0, cw) for c0 in range(0, tn, cw)]
    assert nk == 1 or len(chunks) == 1
    grid = (m // tm, n // tn, nk)
    w_mode = dict(pipeline_mode=pl.Buffered(1)) if resident_w else {}
    assert not resident_w or (n == tn and nk == 1)
    if w.ndim == 3:
        w_spec = pl.BlockSpec((None, tk, tn), lambda i, j, k: (layer, k, j), **w_mode)
    else:
        w_spec = pl.BlockSpec((tk, tn), lambda i, j, k: (k, j), **w_mode)
    in_specs = [pl.BlockSpec((tm, tk), lambda i, j, k: (i, k)), w_spec]
    extras = list(extras)
    if row_ss is not None:
        extras.append((row_ss, (tm, LANES), lambda i, j: (i, 0)))
        epi = _row_scaled(epi, kdim)
    for _, bshape, imap in extras:
        in_specs.append(pl.BlockSpec(bshape, lambda i, j, k, imap=imap: imap(i, j)))
    out_specs = [pl.BlockSpec((tm, o[1]), (lambda i, j, k: (i, 0)) if o[3:] == ("row",) else
                              (lambda i, j, k: (i, j))) for o in outs]
    out_shape = [jax.ShapeDtypeStruct((m, o[0]), o[2]) for o in outs]
    nj = n // tn
    for arr, lyr in casts:
        rows, cols = arr.shape[-2:]
        br = _cast_rows(rows, grid[0] * nj * nk)
        nb = rows // br
        blk = lambda i, j, k, nb=nb: jnp.minimum((i * nj + j) * nk + k, nb - 1)
        in_specs.append(pl.BlockSpec((None, br, cols), lambda i, j, k, lyr=lyr, blk=blk: (lyr, blk(i, j, k), 0)))
        out_specs.append(pl.BlockSpec((br, cols), lambda i, j, k, blk=blk: (blk(i, j, k), 0)))
        out_shape.append(jax.ShapeDtypeStruct((rows, cols), BF16))
    revisits = casts or any(o[3:] == ("row",) for o in outs)
    sem = ("arbitrary",) * 3 if revisits else ("parallel", "parallel", "arbitrary")
    return pl.pallas_call(
        functools.partial(_mm_kernel, epi=epi, n_extra=len(extras), n_cast=len(casts), n_out=len(outs),
                          nk=nk, chunks=chunks),
        grid=grid,
        in_specs=in_specs,
        out_specs=out_specs,
        out_shape=out_shape,
        compiler_params=_params(*sem),
        name=name,
    )(a, w, *[e[0] for e in extras], *[c[0] for c in casts])


def _cols(c0, acc):
    return slice(c0, c0 + acc.shape[1])


def _epi_cast(acc, c0, extras, outs):
    outs[0][:, _cols(c0, acc)] = acc.astype(outs[0].dtype)


def _epi_relu2(acc, c0, extras, outs):
    r = jnp.maximum(acc, 0.0)
    outs[0][:, _cols(c0, acc)] = (r * r).astype(outs[0].dtype)


def _epi_residual(acc, c0, extras, outs):
    cs = _cols(c0, acc)
    outs[0][:, cs] = extras[0][:, cs] + acc


def _epi_residual_join(acc, c0, extras, outs, *, nb_a):
    cs = _cols(c0, acc)
    i = pl.program_id(0)

    @pl.when(i < nb_a)
    def _():
        outs[0][:, cs] = extras[0][:, cs] + acc

    @pl.when(i >= nb_a)
    def _():
        outs[0][:, cs] = extras[1][:, cs] + acc


def _row_sumsq(x):
    return jnp.broadcast_to(jnp.sum(x * x, axis=-1, keepdims=True), (x.shape[0], LANES))


def _epi_residual_emit(acc, c0, extras, outs, *, nb_a=None):
    assert c0 == 0 and acc.shape[1] == outs[0].shape[1]
    if nb_a is None:
        _epi_residual(acc, c0, extras, outs)
    else:
        _epi_residual_join(acc, c0, extras, outs, nb_a=nb_a)
    x = outs[0][...]
    outs[1][...] = (x * extras[-1][...]).astype(outs[1].dtype)
    part = _row_sumsq(x)
    j = pl.program_id(1)

    @pl.when(j == 0)
    def _():
        outs[2][...] = part

    @pl.when(j > 0)
    def _():
        outs[2][...] += part


def _row_scaled(epi, d):
    def wrapped(acc, c0, extras, outs):
        r = lax.rsqrt(extras[-1][...] * (1.0 / d) + EPS)
        epi(acc * jnp.tile(r, (1, acc.shape[1] // LANES)), c0, extras[:-1], outs)
    return wrapped


def _rope(x, c, s):
    return x * c + pltpu.roll(x, LANES // 2, 1) * s


def _epi_latent(acc, c0, extras, outs, *, q_lora, kv_lora):
    gq, gkv, c, s = extras
    if c0 == 0:
        outs[0][...] = _rms(acc, gq[...]).astype(BF16)
    elif c0 == q_lora:
        outs[1][...] = _rms(acc, gkv[...]).astype(BF16)
    else:
        outs[2][...] = _rope(acc, c[...], s[...]).astype(BF16)


def _epi_gqa_heads(acc, c0, extras, outs):
    g, c, sa, sb, ones_bd = extras
    cv, sav, sbv = c[...], sa[...], sb[...]
    quarter = GQA_HEAD_DIM // 4
    for g0 in range(0, acc.shape[1], MXU_COLS):
        x = acc[:, g0:g0 + MXU_COLS]
        x2 = x * x
        hi = x2.astype(BF16)
        lo = (x2 - hi.astype(F32)).astype(BF16)
        ss = jnp.dot(jnp.concatenate([hi, lo], axis=1), ones_bd[...], preferred_element_type=F32)
        y = (x * lax.rsqrt(ss * (1.0 / GQA_HEAD_DIM) + EPS)) * g[:, c0 + g0:c0 + g0 + MXU_COLS]
        for h0 in range(0, MXU_COLS, GQA_HEAD_DIM):
            osl = slice(c0 + g0 + h0, c0 + g0 + h0 + GQA_HEAD_DIM)
            yh = y[:, h0:h0 + GQA_HEAD_DIM]
            r = yh * cv + pltpu.roll(yh, LANES - quarter, 1) * sav + pltpu.roll(yh, quarter, 1) * sbv
            outs[0][:, osl] = r.astype(BF16)


def _epi_mla_q(acc, c0, extras, outs, *, scale):
    c, s = extras
    cv, sv = c[...], s[...]
    for h in range(acc.shape[1] // MLA_QK_PAD):
        lo = slice(h * MLA_QK_PAD, h * MLA_QK_PAD + LANES)
        hi = slice(h * MLA_QK_PAD + LANES, (h + 1) * MLA_QK_PAD)
        olo = slice(c0 + lo.start, c0 + lo.stop)
        ohi = slice(c0 + hi.start, c0 + hi.stop)
        outs[0][:, olo] = (acc[:, lo] * scale).astype(BF16)
        outs[0][:, ohi] = _rope(acc[:, hi], cv, sv).astype(BF16)


def _attn_kernel(q_ref, *refs, n_k, k_per_head, v_per_head, hps, rows):
    k_refs, v_ref, o_ref = refs[:n_k], refs[n_k], refs[n_k + 1]
    qc = q_ref.shape[1] // hps
    dv = o_ref.shape[1] // hps

    def head_kv(hh):
        parts = [r[:, hh * LANES:(hh + 1) * LANES] if ph else r[...] for r, ph in zip(k_refs, k_per_head)]
        k = parts[0] if len(parts) == 1 else jnp.concatenate(parts, axis=-1)
        v = v_ref[:, hh * dv:(hh + 1) * dv] if v_per_head else v_ref[...]
        return k, jnp.concatenate([v, jnp.ones_like(v)], axis=-1)

    shared = not (any(k_per_head) or v_per_head)
    if shared:
        k, v1 = head_kv(0)
    for hh in range(hps):
        if not shared:
            k, v1 = head_kv(hh)
        for r0 in range(0, q_ref.shape[0], rows):
            q = q_ref[r0:r0 + rows, hh * qc:(hh + 1) * qc]
            s = lax.dot_general(q, k, (((1,), (1,)), ((), ())), preferred_element_type=F32)
            m = jnp.max(s, axis=-1, keepdims=True)
            p = jnp.exp2(s - m).astype(BF16)
            o = jnp.dot(p, v1, preferred_element_type=F32)
            o_ref[r0:r0 + rows, hh * dv:(hh + 1) * dv] = (o[:, :dv] / o[:, dv:]).astype(o_ref.dtype)


def _attention(q, q_cols, ks, v, v_spec, *, n_batch, n_tok, n_heads, dv, hps, row_chunk, name):
    tq = _tile(n_tok, ATTN_Q_BLOCK)
    nq = n_tok // tq
    rows = min(tq, row_chunk)
    assert n_heads % hps == 0
    in_specs = [pl.BlockSpec((tq, hps * q_cols), lambda b, h, i: (b * nq + i, h))]
    args = [q]
    for arr, per_head, colfn in ks:
        cols = hps * LANES if per_head else LANES
        in_specs.append(pl.BlockSpec((n_tok, cols), lambda b, h, i, colfn=colfn: (b, colfn(h))))
        args.append(arr)
    v_per_head, v_colfn = v_spec
    in_specs.append(pl.BlockSpec((n_tok, hps * dv if v_per_head else dv), lambda b, h, i: (b, v_colfn(h))))
    args.append(v)
    return pl.pallas_call(
        functools.partial(_attn_kernel, n_k=len(ks), k_per_head=tuple(k[1] for k in ks),
                          v_per_head=v_per_head, hps=hps, rows=rows),
        grid=(n_batch, n_heads // hps, nq),
        in_specs=in_specs,
        out_specs=pl.BlockSpec((tq, hps * dv), lambda b, h, i: (b * nq + i, h)),
        out_shape=jax.ShapeDtypeStruct((n_batch * n_tok, n_heads * dv), BF16),
        compiler_params=_params("parallel", "parallel", "parallel"),
        name=name,
    )(*args)


def _fnet_chan_kernel(u_ref, w_ref, yc_ref, ys_ref):
    gd = yc_ref.shape[1]
    y = jnp.dot(u_ref[...], w_ref[...], preferred_element_type=F32)
    yc_ref[...] = y[:, :gd].astype(BF16)
    ys_ref[...] = y[:, gd:].astype(BF16)


def _fnet_pos_kernel(cn_ref, sn_ref, yc_ref, ys_ref, o_ref):
    o = jnp.dot(cn_ref[...], yc_ref[...], preferred_element_type=F32)
    o = o + jnp.dot(sn_ref[...], ys_ref[...], preferred_element_type=F32)
    o_ref[...] = o.astype(o_ref.dtype)


def _dft_tables(n):
    idx = jnp.arange(n, dtype=jnp.int32)
    jk = (idx[:, None] * idx[None, :]) % n
    ang = jk.astype(F32) * (2.0 * math.pi / n)
    return jnp.cos(ang), jnp.sin(ang)


def _fnet_tables(n_tok, gd):
    norm = 1.0 / math.sqrt(n_tok * gd)
    s_chan = 2.0 ** round(math.log2(norm) / 2)
    s_pos = norm / s_chan
    cc, sc = _dft_tables(gd)
    w_chan = (jnp.concatenate([cc, sc], axis=1) * s_chan).astype(BF16)
    cn, sn = _dft_tables(n_tok)
    return w_chan, (cn * s_pos).astype(BF16), (sn * (-s_pos)).astype(BF16)


def _fnet(z, u_col0, tables, *, n_batch, n_tok, gd):
    t = z.shape[0]
    width = FNET_GROUPS * gd
    w_chan, cn, msn = tables
    tm = _tile(t, 2048)
    ublk = u_col0 // gd
    assert u_col0 % gd == 0
    yc, ys = pl.pallas_call(
        _fnet_chan_kernel,
        grid=(t // tm, FNET_GROUPS),
        in_specs=[pl.BlockSpec((tm, gd), lambda i, g: (i, ublk + g)),
                  pl.BlockSpec((gd, 2 * gd), lambda i, g: (0, 0))],
        out_specs=[pl.BlockSpec((tm, gd), lambda i, g: (i, g))] * 2,
        out_shape=[jax.ShapeDtypeStruct((t, width), BF16)] * 2,
        compiler_params=_params("parallel", "parallel"),
        name="fnet_chan",
    )(z, w_chan)

    tp = _tile(n_tok, 1024)
    tn = _tile(width, 1024)
    npb = n_tok // tp
    return pl.pallas_call(
        _fnet_pos_kernel,
        grid=(n_batch, width // tn, npb),
        in_specs=[pl.BlockSpec((tp, n_tok), lambda b, j, i: (i, 0)),
                  pl.BlockSpec((tp, n_tok), lambda b, j, i: (i, 0)),
                  pl.BlockSpec((n_tok, tn), lambda b, j, i: (b, j)),
                  pl.BlockSpec((n_tok, tn), lambda b, j, i: (b, j))],
        out_specs=pl.BlockSpec((tp, tn), lambda b, j, i: (b * npb + i, j)),
        out_shape=jax.ShapeDtypeStruct((t, width), BF16),
        compiler_params=_params("parallel", "parallel", "parallel"),
        name="fnet_pos",
    )(cn, msn, yc, ys)


def _merge_kernel(oa_ref, ob_ref, oc_ref, w_ref, ga_ref, gb_ref, gc_ref, out_ref):
    tn = out_ref.shape[1]
    cw = min(tn, MXU_COLS)
    for c0 in range(0, tn, cw):
        cs = slice(c0, c0 + cw)
        acc = None
        for b, (o_ref, g_ref) in enumerate(((oa_ref, ga_ref), (ob_ref, gb_ref), (oc_ref, gc_ref))):
            gate = 1.0 / (1.0 + jnp.exp(-g_ref[:, cs].astype(F32)))
            c = gate * jnp.dot(o_ref[...], w_ref[b, :, cs], preferred_element_type=F32)
            acc = c if acc is None else acc + c
        out_ref[:, cs] = acc.astype(out_ref.dtype)


def _merge(oa, ob, oc, w_branch, gates, d_model):
    t, bw = oa.shape
    tm = _tile(t, MERGE_ROWS)
    tn = _tile(d_model, MERGE_COLS)
    nj = d_model // tn
    o_spec = pl.BlockSpec((tm, bw), lambda i, j: (i, 0))
    g_specs = [pl.BlockSpec((tm, tn), lambda i, j, b=b: (i, b * nj + j)) for b in range(N_BRANCH)]
    return pl.pallas_call(
        _merge_kernel,
        grid=(t // tm, nj),
        in_specs=[o_spec, o_spec, o_spec,
                  pl.BlockSpec((N_BRANCH, bw, tn), lambda i, j: (0, 0, j))] + g_specs,
        out_specs=pl.BlockSpec((tm, tn), lambda i, j: (i, j)),
        out_shape=jax.ShapeDtypeStruct((t, d_model), BF16),
        compiler_params=_params("parallel", "parallel"),
        name="gated_merge",
    )(oa, ob, oc, w_branch, gates, gates, gates)


def _rope_tables(n_tok, rot_dim, scale):
    rows = n_tok // GRID_W
    row_idx = jnp.broadcast_to(jnp.arange(rows)[:, None], (rows, GRID_W)).reshape(-1).astype(F32)
    col_idx = jnp.broadcast_to(jnp.arange(GRID_W)[None, :], (rows, GRID_W)).reshape(-1).astype(F32)
    nq = rot_dim // 4
    freqs = ROPE_THETA ** (-(2.0 * jnp.arange(nq, dtype=F32)) / (rot_dim // 2))
    ang = jnp.stack([row_idx[:, None] * freqs, col_idx[:, None] * freqs], axis=1)
    cos, sin = jnp.cos(ang), jnp.sin(ang)
    pad = jnp.zeros((n_tok, LANES // 2 - 2 * nq), F32)
    cblk = jnp.concatenate([cos[:, 0], cos[:, 1], pad], axis=1)
    sblk = jnp.concatenate([sin[:, 0], sin[:, 1], pad], axis=1)
    c = jnp.concatenate([cblk, cblk], axis=1)
    s = jnp.concatenate([-sblk, sblk], axis=1)
    return c * scale, s * scale


def _rope_tables_plain(n_tok, rot_dim):
    rows = n_tok // GRID_W
    row_idx = jnp.broadcast_to(jnp.arange(rows)[:, None], (rows, GRID_W)).reshape(-1).astype(F32)
    col_idx = jnp.broadcast_to(jnp.arange(GRID_W)[None, :], (rows, GRID_W)).reshape(-1).astype(F32)
    nq = rot_dim // 4
    freqs = ROPE_THETA ** (-(2.0 * jnp.arange(nq, dtype=F32)) / (rot_dim // 2))
    ang = jnp.stack([row_idx[:, None] * freqs, col_idx[:, None] * freqs], axis=1)
    cos, sin = jnp.cos(ang), jnp.sin(ang)
    zero = jnp.zeros_like(cos[:, 0])
    c = jnp.concatenate([cos[:, 0], cos[:, 0], cos[:, 1], cos[:, 1]], axis=1)
    sa = jnp.concatenate([-sin[:, 0], zero, -sin[:, 1], zero], axis=1)
    sb = jnp.concatenate([zero, sin[:, 0], zero, sin[:, 1]], axis=1)
    return c, sa, sb


def _rotary_lanes(w, rot_dim):
    nq = rot_dim // 4
    lead = w.shape[:-1]
    r = jnp.swapaxes(w.reshape(*lead, 2, 2, nq), -3, -2).reshape(*lead, 2, 2 * nq)
    r = jnp.pad(r, [(0, 0)] * (len(lead) + 1) + [(0, LANES // 2 - 2 * nq)])
    return r.reshape(*lead, LANES)


def _layer_weights(w_in, w_uq, w_ukv, g_qk_q, g_qk_k, q_lora, kv_lora, d_model, fw, gqa_scale):
    seg = [q_lora, kv_lora, MLA_ROPE, fw, GQA_HEADS * GQA_HEAD_DIM,
           GQA_KV_HEADS * GQA_HEAD_DIM, GQA_KV_HEADS * GQA_HEAD_DIM, N_BRANCH * d_model]
    offs = [0]
    for s in seg:
        offs.append(offs[-1] + s)
    cq, ckv, kpe, uf, qc, kc, vc, gl = [w_in[:, offs[i]:offs[i + 1]] for i in range(8)]
    w_gate = gl.astype(BF16)
    w_uv = jnp.concatenate([uf, vc], axis=1).astype(BF16)
    w_qk = jnp.concatenate([qc, kc], axis=1).astype(BF16)
    w_lat = jnp.concatenate([cq, ckv, _rotary_lanes(kpe, MLA_ROPE)], axis=1).astype(BF16)
    g_qk = jnp.concatenate([jnp.tile(g_qk_q * gqa_scale, GQA_HEADS),
                            jnp.tile(g_qk_k, GQA_KV_HEADS)]).reshape(1, -1)
    uq = w_uq.reshape(q_lora, MLA_HEADS, MLA_NOPE + MLA_ROPE)
    uq = jnp.concatenate([uq[:, :, :MLA_NOPE], _rotary_lanes(uq[:, :, MLA_NOPE:], MLA_ROPE)], axis=2)
    w_uq_p = uq.reshape(q_lora, MLA_HEADS * MLA_QK_PAD).astype(BF16)
    ukv = w_ukv.reshape(kv_lora, MLA_HEADS, MLA_NOPE + MLA_V)
    w_ukv_p = jnp.concatenate([ukv[:, :, :MLA_NOPE].reshape(kv_lora, -1),
                               ukv[:, :, MLA_NOPE:].reshape(kv_lora, -1)], axis=1).astype(BF16)
    return w_gate, w_uv, w_qk, w_lat, g_qk, w_uq_p, w_ukv_p


def _layer(x, n_batch, n_tok, tabs, fnet_tabs, w_in, g_attn, g_qa, w_uq, g_kva, w_ukv,
           g_qk_q, g_qk_k, g_mlp, fw, own, gate_casts, up_casts):
    joined = isinstance(x, tuple)
    d_model = g_attn.shape[0]
    q_lora, kv_lora = g_qa.shape[0], g_kva.shape[0]
    gd = fw // FNET_GROUPS
    mla_scale = (MLA_NOPE + MLA_ROPE) ** -0.5 * LOG2E
    gqa_scale = GQA_HEAD_DIM ** -0.5 * LOG2E
    (ca, sa), (ca_s, sa_s), (cc, sac, sbc), ones_bd = tabs
    w_gate, w_uv, w_qk, w_lat, g_qk, w_uq_p, w_ukv_p = _layer_weights(
        w_in, w_uq, w_ukv, g_qk_q, g_qk_k, q_lora, kv_lora, d_model, fw, gqa_scale)

    tm = _tile(n_tok, 1024)
    nrb = n_tok // tm
    tab_map = lambda i, j: (i % nrb, 0)

    if joined:
        h = _rmsnorm_join(x[0], x[1], g_attn, BF16, "rmsnorm_attn")
    else:
        h = _rmsnorm(x, g_attn, BF16, "rmsnorm_attn")

    gates, *cast_out = _mm(h, w_gate, _epi_cast, [(w_gate.shape[1], _tile(w_gate.shape[1], 1024), BF16)],
                           tm=tm, tn=1024, casts=gate_casts, name="in_proj_gates")
    if own is None:
        wb2d, wo_bf, wup_bf, wdown_bf = cast_out
        wb_bf = wb2d.reshape(N_BRANCH, fw, d_model)
    else:
        wb_bf, wo_bf, wup_bf, wdown_bf = own
    (uv,) = _mm(h, w_uv, _epi_cast, [(w_uv.shape[1], _tile(w_uv.shape[1], 1280), BF16)],
                tm=tm, tn=1280, name="in_proj_uv")
    qk_tn = _tile(w_qk.shape[1], 512)
    (qk,) = _mm(h, w_qk, _epi_gqa_heads, [(w_qk.shape[1], qk_tn, BF16)],
                extras=[(g_qk, (1, qk_tn), lambda i, j: (0, j)),
                        (cc, (tm, LANES), tab_map), (sac, (tm, LANES), tab_map), (sbc, (tm, LANES), tab_map),
                        (ones_bd, ones_bd.shape, lambda i, j: (0, 0))],
                tm=tm, tn=512, name="in_proj_gqa_qk")
    cqn, ckvn, kpe = _mm(
        h, w_lat, functools.partial(_epi_latent, q_lora=q_lora, kv_lora=kv_lora),
        [(q_lora, q_lora, BF16), (kv_lora, kv_lora, BF16), (LANES, LANES, BF16)],
        extras=[(g_qa.reshape(1, -1), (1, q_lora), lambda i, j: (0, 0)),
                (g_kva.reshape(1, -1), (1, kv_lora), lambda i, j: (0, 0)),
                (ca, (tm, LANES), tab_map), (sa, (tm, LANES), tab_map)],
        tm=tm, tn=w_lat.shape[1], resident_w=True,
        chunks=[(0, q_lora), (q_lora, kv_lora), (q_lora + kv_lora, LANES)], name="in_proj_latent")

    (q_a,) = _mm(cqn, w_uq_p, functools.partial(_epi_mla_q, scale=mla_scale),
                 [(w_uq_p.shape[1], w_uq_p.shape[1], BF16)],
                 extras=[(ca_s, (tm, LANES), tab_map), (sa_s, (tm, LANES), tab_map)],
                 tm=tm, tn=w_uq_p.shape[1], chunk=2 * MLA_QK_PAD, resident_w=True, name="mla_q_up")
    (kv_a,) = _mm(ckvn, w_ukv_p, _epi_cast, [(w_ukv_p.shape[1], w_ukv_p.shape[1], BF16)],
                  tm=tm, tn=w_ukv_p.shape[1], chunk=1024, resident_w=True, name="mla_kv_up")
    o_a = _attention(q_a, MLA_QK_PAD,
                     [(kv_a, True, lambda hs: hs), (kpe, False, lambda hs: 0)],
                     kv_a, (True, lambda hs: MLA_HEADS // MLA_HEADS_PER_STEP + hs),
                     n_batch=n_batch, n_tok=n_tok, n_heads=MLA_HEADS, dv=MLA_V, hps=MLA_HEADS_PER_STEP,
                     row_chunk=MLA_ROW_CHUNK, name="mla_attention")

    o_b = _fnet(uv, 0, fnet_tabs, n_batch=n_batch, n_tok=n_tok, gd=gd)

    group = GQA_HEADS // GQA_KV_HEADS
    hps = GQA_HEADS_PER_STEP
    assert group % hps == 0
    o_c = _attention(qk, GQA_HEAD_DIM,
                     [(qk, False, lambda hs: GQA_HEADS + (hs * hps) // group)],
                     uv, (False, lambda hs: fw // GQA_HEAD_DIM + (hs * hps) // group),
                     n_batch=n_batch, n_tok=n_tok, n_heads=GQA_HEADS, dv=GQA_HEAD_DIM, hps=hps,
                     row_chunk=GQA_ROW_CHUNK, name="gqa_attention")

    merged = _merge(o_a, o_b, o_c, wb_bf, gates, d_model)
    otn = _tile(d_model, 512)
    if joined:
        nb_a = x[0].shape[0] // tm
        res_extras = [(x[0], (tm, otn), lambda i, j: (jnp.minimum(i, nb_a - 1), j)),
                      (x[1], (tm, otn), lambda i, j: (jnp.maximum(i - nb_a, 0), j))]
    else:
        nb_a = None
        res_extras = [(x, (tm, otn), lambda i, j: (i, j))]
    ss_out = (LANES, LANES, F32, "row")
    x, h2, ss2 = _mm(merged, wo_bf, functools.partial(_epi_residual_emit, nb_a=nb_a),
                     [(d_model, otn, F32), (d_model, otn, BF16), ss_out],
                     extras=res_extras + [(g_mlp.reshape(1, -1), (1, otn), lambda i, j: (0, j))],
                     tm=tm, tn=512, name="out_proj")

    d_ff = wup_bf.shape[1]
    act, *next_weights = _mm(h2, wup_bf, _epi_relu2, [(d_ff, _tile(d_ff, 1024), BF16)],
                             tm=tm, tn=1024, casts=up_casts, row_ss=ss2, name="mlp_up")
    dtn = _tile(d_model, 1024)
    (x,) = _mm(act, wdown_bf, _epi_residual, [(d_model, dtn, F32)],
               extras=[(x, (tm, dtn), lambda i, j: (i, j))],
               tm=tm, tn=1024, tk=4096, name="mlp_down")
    return x, next_weights


def kernel(x_prompt, x_sample, w_in, g_attn, g_qa, w_uq, g_kva, w_ukv, g_qk_q, g_qk_k,
           w_branch, w_o, g_mlp, w_up, w_down, g_final):
    n_tok, d_model = x_prompt.shape[1], x_prompt.shape[2]
    assert x_sample.shape[1:] == (n_tok, d_model)
    b_p, b_s = x_prompt.shape[0], x_sample.shape[0]
    n_batch = b_p + b_s
    x = (x_prompt.reshape(b_p * n_tok, d_model), x_sample.reshape(b_s * n_tok, d_model))

    mla_scale = (MLA_NOPE + MLA_ROPE) ** -0.5 * LOG2E
    heads_per_chunk = MXU_COLS // GQA_HEAD_DIM
    ones_bd = jnp.kron(jnp.eye(heads_per_chunk, dtype=F32), jnp.ones((GQA_HEAD_DIM, GQA_HEAD_DIM), F32))
    ones_bd = jnp.tile(ones_bd, (2, 1)).astype(BF16)
    tabs = (_rope_tables(n_tok, MLA_ROPE, 1.0), _rope_tables(n_tok, MLA_ROPE, mla_scale),
            _rope_tables_plain(n_tok, GQA_HEAD_DIM), ones_bd)
    n_layers, _, fw, _ = w_branch.shape
    fnet_tabs = _fnet_tables(n_tok, fw // FNET_GROUPS)

    wb_rows = w_branch.reshape(n_layers, N_BRANCH * fw, d_model)
    layer_casts = lambda l: [(wb_rows, l), (w_o, l), (w_up, l), (w_down, l)]
    own, gate_casts = None, layer_casts(0)
    for l in range(n_layers):
        up_casts = layer_casts(l + 1) if l + 1 < n_layers else []
        x, nxt = _layer(x, n_batch, n_tok, tabs, fnet_tabs, w_in[l], g_attn[l], g_qa[l], w_uq[l], g_kva[l],
                        w_ukv[l], g_qk_q[l], g_qk_k[l], g_mlp[l], fw, own, gate_casts, up_casts)
        if nxt:
            own, gate_casts = (nxt[0].reshape(N_BRANCH, fw, d_model), *nxt[1:]), []
    y_p, y_s = _rmsnorm_split(x, g_final, b_p * n_tok, F32, "rmsnorm_final")
    return (y_p.reshape(b_p, n_tok, d_model), y_s.reshape(b_s, n_tok, d_model))
```
